```python
import math
import jax
import jax.numpy as jnp
from jax import lax
import numpy as np

D_MODEL = 1024
BATCH = 8
SEQ = 2048
DEPTH = 4
DEC_BATCH = 8
DEC_SEQ = 4096
PAST_LEN = 128

GRID_W = 64
N_MEM = 256
BLOCK = 128
EPS = 1e-6
ROPE_THETA = 10000.0
NEG_INF = -1e30

A_HEADS = 8
A_KV_HEADS = 2
A_HEAD_DIM = 64
A_GROUP = A_HEADS // A_KV_HEADS
A_Q = A_HEADS * A_HEAD_DIM
A_KV = A_KV_HEADS * A_HEAD_DIM
B_HEADS = 8
B_Q_RANK = 384
B_KV_RANK = 256
B_NOPE = 64
B_ROPE = 32
B_V = 64
B_QK = B_NOPE + B_ROPE
C_HEADS = 8
C_KV_HEADS = 2
C_HEAD_DIM = 64
C_GROUP = C_HEADS // C_KV_HEADS
C_WINDOW = 128
C_Q = C_HEADS * C_HEAD_DIM
C_KV = C_KV_HEADS * C_HEAD_DIM
X_HEADS = 4
X_HEAD_DIM = 64
X_Q = X_HEADS * X_HEAD_DIM
N_BRANCH = 4
N_EXPERTS = 16
EC_FACTOR = 2
D_EXPERT = 1024

A_WIDTH = A_Q + 2 * A_KV
B_WIDTH = B_Q_RANK + B_KV_RANK + B_ROPE
C_WIDTH = C_Q + 2 * C_KV
X_WIDTH = X_Q
IN_A = 0
IN_B = IN_A + A_WIDTH
IN_C = IN_B + B_WIDTH
IN_X = IN_C + C_WIDTH
IN_G = IN_X + X_WIDTH
IN_WIDTH = IN_G + N_BRANCH * D_MODEL
BR_A = 0
BR_B = BR_A + A_Q
BR_C = BR_B + B_HEADS * B_V
BR_X = BR_C + C_Q
BRANCH_WIDTH = BR_X + X_Q

kernel_name = 'hybrid_gated_parallel_encoder'


def rms_norm(x, g):
    xf = x.astype(jnp.float32)
    y = xf * lax.rsqrt(jnp.mean(xf * xf, axis=-1, keepdims=True) + EPS)
    return (y * g.astype(jnp.float32)).astype(x.dtype)


def rope_angles(pos, dim):
    inv = jnp.power(ROPE_THETA, -jnp.arange(0, dim, 2, dtype=jnp.float32) / dim)
    return pos.astype(jnp.float32)[:, None] * inv[None, :]


def apply_rope(x, ang):
    half = x.shape[-1] // 2
    x1, x2 = x[..., :half], x[..., half:]
    cos = jnp.cos(ang)[:, None, :].astype(x.dtype)
    sin = jnp.sin(ang)[:, None, :].astype(x.dtype)
    return jnp.concatenate([x1 * cos - x2 * sin, x1 * sin + x2 * cos], axis=-1)


def axial_rope(x, row_ang, col_ang):
    half = x.shape[-1] // 2
    return jnp.concatenate([apply_rope(x[..., :half], row_ang), apply_rope(x[..., half:], col_ang)], axis=-1)


def alibi_slopes(n_heads):
    return jnp.power(2.0, -8.0 * jnp.arange(1, n_heads + 1, dtype=jnp.float32) / n_heads)


def dense_block_attention(q, k, v, scale):
    B, S, Hkv, G, dk = q.shape
    nb = S // BLOCK
    qb = q.reshape(B, nb, BLOCK, Hkv, G, dk).transpose(1, 0, 2, 3, 4, 5)

    def one_block(qblk):
        s = jnp.einsum('bqhgd,bkhd->bhgqk', qblk, k, preferred_element_type=jnp.float32) * scale
        p = jax.nn.softmax(s, axis=-1).astype(v.dtype)
        return jnp.einsum('bhgqk,bkhd->bqhgd', p, v)

    ob = lax.map(one_block, qb)
    return ob.transpose(1, 0, 2, 3, 4, 5).reshape(B, S, Hkv, G, v.shape[-1])


def window_block_attention(q, k, v, slopes, sink, scale):
    B, S, Hkv, G, d = q.shape
    nb = S // BLOCK
    pad = [(0, 0), (BLOCK, BLOCK), (0, 0), (0, 0)]
    kp = jnp.pad(k, pad)
    vp = jnp.pad(v, pad)
    qb = q.reshape(B, nb, BLOCK, Hkv, G, d).transpose(1, 0, 2, 3, 4, 5)
    i = jnp.arange(BLOCK, dtype=jnp.int32)
    u = jnp.arange(3 * BLOCK, dtype=jnp.int32)
    rel = (u[None, :] - BLOCK) - i[:, None]
    dist = jnp.abs(rel).astype(jnp.float32)
    penalty = slopes.astype(jnp.float32)[None, :, :, None, None] * dist[None, None, None]
    sink_col = jnp.broadcast_to(sink.astype(jnp.float32)[None, :, :, None, None], (B, Hkv, G, BLOCK, 1))

    def one_block(args):
        j, qblk = args
        kb = lax.dynamic_slice_in_dim(kp, j * BLOCK, 3 * BLOCK, axis=1)
        vb = lax.dynamic_slice_in_dim(vp, j * BLOCK, 3 * BLOCK, axis=1)
        s_pos = j * BLOCK - BLOCK + u
        valid = (jnp.abs(rel) <= C_WINDOW) & ((s_pos >= 0) & (s_pos < S))[None, :]
        s = jnp.einsum('bqhgd,bkhd->bhgqk', qblk, kb, preferred_element_type=jnp.float32) * scale - penalty
        s = jnp.where(valid[None, None, None], s, NEG_INF)
        p = jax.nn.softmax(jnp.concatenate([s, sink_col], axis=-1), axis=-1)[..., :-1].astype(v.dtype)
        return jnp.einsum('bhgqk,bkhd->bqhgd', p, vb)

    ob = lax.map(one_block, (jnp.arange(nb, dtype=jnp.int32), qb))
    return ob.transpose(1, 0, 2, 3, 4, 5).reshape(B, S, Hkv, G, d)


def mixer_a(z, row_ang, col_ang, q_g, k_g):
    B, S, _ = z.shape
    q = z[..., :A_Q].reshape(B, S, A_HEADS, A_HEAD_DIM)
    k = z[..., A_Q:A_Q + A_KV].reshape(B, S, A_KV_HEADS, A_HEAD_DIM)
    v = z[..., A_Q + A_KV:].reshape(B, S, A_KV_HEADS, A_HEAD_DIM)
    q = axial_rope(rms_norm(q, q_g), row_ang, col_ang)
    k = axial_rope(rms_norm(k, k_g), row_ang, col_ang)
    q = q.reshape(B, S, A_KV_HEADS, A_GROUP, A_HEAD_DIM)
    o = dense_block_attention(q, k, v, A_HEAD_DIM ** -0.5)
    return o.reshape(B, S, A_Q)


def mixer_b(z, seq_ang, q_a_g, w_q_b, kv_a_g, w_kv_b, q_g, k_g):
    B, S, _ = z.shape
    c_q = z[..., :B_Q_RANK]
    c_kv = z[..., B_Q_RANK:B_Q_RANK + B_KV_RANK]
    k_rope = z[..., B_Q_RANK + B_KV_RANK:]
    q = (rms_norm(c_q, q_a_g) @ w_q_b).reshape(B, S, B_HEADS, B_QK)
    kv = (rms_norm(c_kv, kv_a_g) @ w_kv_b).reshape(B, S, B_HEADS, B_NOPE + B_V)
    k_nope, v = kv[..., :B_NOPE], kv[..., B_NOPE:]
    k = jnp.concatenate([k_nope, jnp.broadcast_to(k_rope[:, :, None, :], (B, S, B_HEADS, B_ROPE))], axis=-1)
    q = rms_norm(q, q_g)
    k = rms_norm(k, k_g)
    q = jnp.concatenate([q[..., :B_NOPE], apply_rope(q[..., B_NOPE:], seq_ang)], axis=-1)
    k = jnp.concatenate([k[..., :B_NOPE], apply_rope(k[..., B_NOPE:], seq_ang)], axis=-1)
    o = dense_block_attention(q[:, :, :, None, :], k, v, B_QK ** -0.5)
    return o.reshape(B, S, B_HEADS * B_V)


def mixer_c(z, slopes, sink, q_g, k_g):
    B, S, _ = z.shape
    q = rms_norm(z[..., :C_Q].reshape(B, S, C_HEADS, C_HEAD_DIM), q_g)
    k = rms_norm(z[..., C_Q:C_Q + C_KV].reshape(B, S, C_KV_HEADS, C_HEAD_DIM), k_g)
    v = z[..., C_Q + C_KV:].reshape(B, S, C_KV_HEADS, C_HEAD_DIM)
    q = q.reshape(B, S, C_KV_HEADS, C_GROUP, C_HEAD_DIM)
    o = window_block_attention(q, k, v, slopes, sink, C_HEAD_DIM ** -0.5)
    return o.reshape(B, S, C_Q)


def memory_cross_attention(zq, mem, mem_g, w_mem_kv, q_g, k_g):
    B, S, _ = zq.shape
    q = rms_norm(zq.reshape(B, S, X_HEADS, X_HEAD_DIM), q_g)
    kv = rms_norm(mem, mem_g) @ w_mem_kv
    k = rms_norm(kv[..., :X_Q].reshape(B, N_MEM, X_HEADS, X_HEAD_DIM), k_g)
    v = kv[..., X_Q:].reshape(B, N_MEM, X_HEADS, X_HEAD_DIM)
    s = jnp.einsum('bqhd,bkhd->bhqk', q, k, preferred_element_type=jnp.float32) * (X_HEAD_DIM ** -0.5)
    p = jax.nn.softmax(s, axis=-1).astype(v.dtype)
    return jnp.einsum('bhqk,bkhd->bqhd', p, v).reshape(B, S, X_Q)


def expert_choice_moe(h, w_router, w_gate_up, w_down):
    B, S, D = h.shape
    n = B * S
    cap = EC_FACTOR * n // N_EXPERTS
    xt = h.reshape(n, D)
    aff = jax.nn.softmax((xt @ w_router).astype(jnp.float32), axis=-1)
    gate, idx = lax.top_k(aff.T, cap)
    xs = xt[idx]
    gu = jnp.einsum('ecd,edf->ecf', xs, w_gate_up)
    g, u = gu[..., :D_EXPERT], gu[..., D_EXPERT:]
    y = jnp.einsum('ecf,efd->ecd', jax.nn.silu(g) * u, w_down) * gate[..., None].astype(h.dtype)
    out = jnp.zeros_like(xt).at[idx.reshape(-1)].add(y.reshape(-1, D))
    return out.reshape(B, S, D)


def encoder_trunk(x, mem, norm1_g, w_in, a_q_norm, a_k_norm, b_q_a_norm, b_w_q_b, b_kv_a_norm, b_w_kv_b,
                  b_q_norm, b_k_norm, c_q_norm, c_k_norm, c_sink, x_q_norm, x_k_norm, mem_norm_g, w_mem_kv,
                  w_branch, w_out, norm2_g, w_router, w_gate_up, w_down):
    B, S, D = x.shape
    rows = S // GRID_W
    row_idx = jnp.repeat(jnp.arange(rows, dtype=jnp.int32), GRID_W)
    col_idx = jnp.tile(jnp.arange(GRID_W, dtype=jnp.int32), rows)
    tok_idx = jnp.arange(S, dtype=jnp.int32)
    row_ang = rope_angles(row_idx, A_HEAD_DIM // 2)
    col_ang = rope_angles(col_idx, A_HEAD_DIM // 2)
    seq_ang = rope_angles(tok_idx, B_ROPE)
    slopes = alibi_slopes(C_HEADS).reshape(C_KV_HEADS, C_GROUP)
    for l in range(DEPTH):
        h = rms_norm(x, norm1_g[l])
        z = h @ w_in[l]
        oa = mixer_a(z[..., IN_A:IN_B], row_ang, col_ang, a_q_norm[l], a_k_norm[l])
        ob = mixer_b(z[..., IN_B:IN_C], seq_ang, b_q_a_norm[l], b_w_q_b[l], b_kv_a_norm[l], b_w_kv_b[l],
                     b_q_norm[l], b_k_norm[l])
        oc = mixer_c(z[..., IN_C:IN_X], slopes, c_sink[l].reshape(C_KV_HEADS, C_GROUP), c_q_norm[l], c_k_norm[l])
        ox = memory_cross_attention(z[..., IN_X:IN_G], mem, mem_norm_g[l], w_mem_kv[l], x_q_norm[l], x_k_norm[l])
        gates = jax.nn.sigmoid(z[..., IN_G:].astype(jnp.float32)).astype(x.dtype).reshape(B, S, N_BRANCH, D)
        wb = w_branch[l]
        merged = (gates[:, :, 0] * (oa @ wb[BR_A:BR_B])
                  + gates[:, :, 1] * (ob @ wb[BR_B:BR_C])
                  + gates[:, :, 2] * (oc @ wb[BR_C:BR_X])
                  + gates[:, :, 3] * (ox @ wb[BR_X:]))
        x = x + merged @ w_out[l]
        x = x + expert_choice_moe(rms_norm(x, norm2_g[l]), w_router[l], w_gate_up[l], w_down[l])
    return x


def setup_inputs(seed: int = 0) -> dict:
    key = jax.random.key(seed)
    ks = jax.random.split(key, 28)
    f32 = jnp.float32

    def nrm(k, shape, fan_in):
        return jax.random.normal(k, shape, f32) * (fan_in ** -0.5)

    def gain(k, shape):
        return 1.0 + 0.01 * jax.random.normal(k, shape, f32)

    return {
        'x_prompt': jax.random.normal(ks[0], (BATCH, SEQ, D_MODEL), f32),
        'x_sample': jax.random.normal(ks[1], (DEC_BATCH, DEC_SEQ, D_MODEL), f32),
        'mem_prompt': jax.random.normal(ks[2], (BATCH, N_MEM, D_MODEL), f32),
        'mem_sample': jax.random.normal(ks[3], (DEC_BATCH, N_MEM, D_MODEL), f32),
        'norm1_g': gain(ks[4], (DEPTH, D_MODEL)),
        'w_in': nrm(ks[5], (DEPTH, D_MODEL, IN_WIDTH), D_MODEL),
        'a_q_norm': gain(ks[6], (DEPTH, A_HEAD_DIM)),
        'a_k_norm': gain(ks[7], (DEPTH, A_HEAD_DIM)),
        'b_q_a_norm': gain(ks[8], (DEPTH, B_Q_RANK)),
        'b_w_q_b': nrm(ks[9], (DEPTH, B_Q_RANK, B_HEADS * B_QK), B_Q_RANK),
        'b_kv_a_norm': gain(ks[10], (DEPTH, B_KV_RANK)),
        'b_w_kv_b': nrm(ks[11], (DEPTH, B_KV_RANK, B_HEADS * (B_NOPE + B_V)), B_KV_RANK),
        'b_q_norm': gain(ks[12], (DEPTH, B_QK)),
        'b_k_norm': gain(ks[13], (DEPTH, B_QK)),
        'c_q_norm': gain(ks[14], (DEPTH, C_HEAD_DIM)),
        'c_k_norm': gain(ks[15], (DEPTH, C_HEAD_DIM)),
        'c_sink': 0.5 * jax.random.normal(ks[16], (DEPTH, C_HEADS), f32),
        'x_q_norm': gain(ks[17], (DEPTH, X_HEAD_DIM)),
        'x_k_norm': gain(ks[18], (DEPTH, X_HEAD_DIM)),
        'mem_norm_g': gain(ks[19], (DEPTH, D_MODEL)),
        'w_mem_kv': nrm(ks[20], (DEPTH, D_MODEL, 2 * X_Q), D_MODEL),
        'w_branch': nrm(ks[21], (DEPTH, BRANCH_WIDTH, D_MODEL), A_Q),
        'w_out': nrm(ks[22], (DEPTH, D_MODEL, D_MODEL), D_MODEL),
        'norm2_g': gain(ks[23], (DEPTH, D_MODEL)),
        'w_router': nrm(ks[24], (DEPTH, D_MODEL, N_EXPERTS), D_MODEL),
        'w_gate_up': nrm(ks[25], (DEPTH, N_EXPERTS, D_MODEL, 2 * D_EXPERT), D_MODEL),
        'w_down': nrm(ks[26], (DEPTH, N_EXPERTS, D_EXPERT, D_MODEL), D_EXPERT),
    }


def reference(x_prompt, x_sample, mem_prompt, mem_sample, norm1_g, w_in, a_q_norm, a_k_norm, b_q_a_norm,
              b_w_q_b, b_kv_a_norm, b_w_kv_b, b_q_norm, b_k_norm, c_q_norm, c_k_norm, c_sink, x_q_norm,
              x_k_norm, mem_norm_g, w_mem_kv, w_branch, w_out, norm2_g, w_router, w_gate_up, w_down):
    y_prompt = encoder_trunk(x_prompt, mem_prompt, norm1_g, w_in, a_q_norm, a_k_norm, b_q_a_norm, b_w_q_b,
                             b_kv_a_norm, b_w_kv_b, b_q_norm, b_k_norm, c_q_norm, c_k_norm, c_sink, x_q_norm,
                             x_k_norm, mem_norm_g, w_mem_kv, w_branch, w_out, norm2_g, w_router, w_gate_up, w_down)
    y_sample = encoder_trunk(x_sample, mem_sample, norm1_g, w_in, a_q_norm, a_k_norm, b_q_a_norm, b_w_q_b,
                             b_kv_a_norm, b_w_kv_b, b_q_norm, b_k_norm, c_q_norm, c_k_norm, c_sink, x_q_norm,
                             x_k_norm, mem_norm_g, w_mem_kv, w_branch, w_out, norm2_g, w_router, w_gate_up, w_down)
    return (y_prompt, y_sample)
```

```python
import functools

import jax
import jax.numpy as jnp
from jax import lax
from jax.experimental import pallas as pl
from jax.experimental.pallas import tpu as pltpu

D_MODEL = 1024
GRID_W = 64
N_MEM = 256
EPS = 1e-6
ROPE_THETA = 10000.0
NEG_INF = -1e30
HEAD = 64
A_KV_HEADS = 2
B_HEADS = 8
B_Q_RANK = 384
B_KV_RANK = 256
B_NOPE = 64
B_ROPE = 32
B_QK = B_NOPE + B_ROPE
C_WINDOW = 128
X_HEADS = 4
N_EXPERTS = 16
EC_FACTOR = 2
D_EXPERT = 1024
LANES = 128

MXU_DTYPE = jnp.bfloat16
VMEM_LIMIT = 56 * 1024 * 1024

_C_AQ, _C_AK, _C_AV = 0, 512, 768
_C_BCQ, _C_BCKV = 1024, 1408
_C_CQ, _C_CK, _C_CV = 1664, 2176, 2432
_C_XQ, _C_KR, _C_END = 2688, 2944, 3072


def _dot(a, b):
    return jnp.dot(a, b, preferred_element_type=jnp.float32)


def _dot_nt(a, b):
    return lax.dot_general(a, b, (((1,), (1,)), ((), ())), preferred_element_type=jnp.float32)


def _rms(x, g):
    return x * lax.rsqrt(jnp.mean(x * x, axis=-1, keepdims=True) + EPS) * g


def _head_norm(z, gain, head_w, n_valid):
    lane = lax.broadcasted_iota(jnp.int32, (1, LANES), 1)
    outs = []
    for c in range(z.shape[1] // LANES):
        blk = z[:, c * LANES:(c + 1) * LANES]
        sq = blk * blk
        if head_w == LANES:
            r = lax.rsqrt(jnp.sum(sq, axis=-1, keepdims=True) * (1.0 / n_valid) + EPS)
        else:
            lo = lane < head_w
            s_lo = jnp.sum(jnp.where(lo, sq, 0.0), axis=-1, keepdims=True)
            s_hi = jnp.sum(jnp.where(lo, 0.0, sq), axis=-1, keepdims=True)
            r = jnp.where(lo, lax.rsqrt(s_lo * (1.0 / n_valid) + EPS), lax.rsqrt(s_hi * (1.0 / n_valid) + EPS))
        outs.append(blk * r * gain)
    return outs


def _rope(blocks, cos, sin_signed):
    lane = lax.broadcasted_iota(jnp.int32, (1, LANES), 1)
    first = (lane % 32) < 16
    outs = []
    for blk in blocks:
        up = pltpu.roll(blk, LANES - 16, 1)
        dn = pltpu.roll(blk, 16, 1)
        outs.append(blk * cos + jnp.where(first, up, dn) * sin_signed)
    return outs


def _cat(blocks, dtype):
    return jnp.concatenate(blocks, axis=1).astype(dtype)


def _inproj_kernel(x_ref, g1_ref, w1_ref, wqb_ref, wkb_ref, wvb_ref, gains_ref, gbq_ref, gbkv_ref,
                   cosa_ref, sina_ref, cosb_ref, sinb_ref,
                   qa_ref, ka_ref, va_ref, qb_ref, kb_ref, vb_ref, qc_ref, kc_ref, vc_ref, qx_ref):
    x = x_ref[...]
    hb = _rms(x, g1_ref[...]).astype(MXU_DTYPE)
    z = _dot(hb, w1_ref[...])
    gains = gains_ref[...]
    cosa, sina = cosa_ref[...], sina_ref[...]
    cosb, sinb = cosb_ref[...], sinb_ref[...]
    od = qa_ref.dtype

    qa = _rope(_head_norm(z[:, _C_AQ:_C_AK], gains[0:1], HEAD, HEAD), cosa, sina)
    qa_ref[...] = _cat(qa, od)
    ka = _rope(_head_norm(z[:, _C_AK:_C_AV], gains[1:2], HEAD, HEAD), cosa, sina)
    ka_ref[...] = _cat([ka[0], ka[0], ka[1], ka[1]], od)
    va = z[:, _C_AV:_C_BCQ]
    va_ref[...] = _cat([va[:, :LANES], va[:, :LANES], va[:, LANES:], va[:, LANES:]], od)

    cq = _rms(z[:, _C_BCQ:_C_BCKV], gbq_ref[...]).astype(MXU_DTYPE)
    ckv = _rms(z[:, _C_BCKV:_C_CQ], gbkv_ref[...]).astype(MXU_DTYPE)
    q8 = _dot(cq, wqb_ref[...])
    k8 = _dot(ckv, wkb_ref[...])
    kr = z[:, _C_KR:_C_END]
    k8 = k8 + jnp.concatenate([kr] * B_HEADS, axis=1)
    qb_ref[...] = _cat(_rope(_head_norm(q8, gains[2:3], LANES, B_QK), cosb, sinb), od)
    kb_ref[...] = _cat(_rope(_head_norm(k8, gains[3:4], LANES, B_QK), cosb, sinb), od)
    vb_ref[...] = _dot(ckv, wvb_ref[...]).astype(od)

    qc_ref[...] = _cat(_head_norm(z[:, _C_CQ:_C_CK], gains[4:5], HEAD, HEAD), od)
    kc = _head_norm(z[:, _C_CK:_C_CV], gains[5:6], HEAD, HEAD)
    kc_ref[...] = _cat([kc[0], kc[0], kc[1], kc[1]], od)
    vc = z[:, _C_CV:_C_XQ]
    vc_ref[...] = _cat([vc[:, :LANES], vc[:, :LANES], vc[:, LANES:], vc[:, LANES:]], od)

    qx_ref[...] = _cat(_head_norm(z[:, _C_XQ:_C_KR], gains[6:7], HEAD, HEAD), od)


def _in_projection(x2, lw, tabs, seq, tm):
    n = x2.shape[0]
    nt = seq // tm
    row = lambda i: (i, 0)
    full = lambda i: (0, 0)
    tab = lambda i: (i % nt, 0)
    widths = [512, 512, 512, 1024, 1024, 512, 512, 512, 512, 256]
    outs = pl.pallas_call(
        _inproj_kernel,
        grid=(n // tm,),
        in_specs=[
            pl.BlockSpec((tm, D_MODEL), row),
            pl.BlockSpec((1, D_MODEL), full),
            pl.BlockSpec(lw["w1"].shape, full),
            pl.BlockSpec(lw["wqb"].shape, full),
            pl.BlockSpec(lw["wkb"].shape, full),
            pl.BlockSpec(lw["wvb"].shape, full),
            pl.BlockSpec(lw["gains"].shape, full),
            pl.BlockSpec((1, B_Q_RANK), full),
            pl.BlockSpec((1, B_KV_RANK), full),
            pl.BlockSpec((tm, LANES), tab),
            pl.BlockSpec((tm, LANES), tab),
            pl.BlockSpec((tm, LANES), tab),
            pl.BlockSpec((tm, LANES), tab),
        ],
        out_specs=[pl.BlockSpec((tm, w), row) for w in widths],
        out_shape=[jax.ShapeDtypeStruct((n, w), MXU_DTYPE) for w in widths],
        compiler_params=pltpu.CompilerParams(dimension_semantics=("parallel",), vmem_limit_bytes=VMEM_LIMIT),
        name="in_projection",
    )(x2, lw["g1"], lw["w1"], lw["wqb"], lw["wkb"], lw["wvb"], lw["gains"], lw["gbq"], lw["gbkv"],
      tabs["cosa"], tabs["sina"], tabs["cosb"], tabs["sinb"])
    return outs


def _memkv_kernel(mem_ref, g_ref, w_ref, gain_ref, k_ref, v_ref):
    m = _rms(mem_ref[...], g_ref[...]).astype(MXU_DTYPE)
    kv = _dot(m, w_ref[...])
    half = kv.shape[1] // 2
    k_ref[...] = _cat(_head_norm(kv[:, :half], gain_ref[...], HEAD, HEAD), k_ref.dtype)
    v_ref[...] = kv[:, half:].astype(v_ref.dtype)


def _memory_kv(mem2, lw):
    n = mem2.shape[0]
    tm = N_MEM
    xq = X_HEADS * HEAD
    row = lambda i: (i, 0)
    full = lambda i: (0, 0)
    return pl.pallas_call(
        _memkv_kernel,
        grid=(n // tm,),
        in_specs=[pl.BlockSpec((tm, D_MODEL), row), pl.BlockSpec((1, D_MODEL), full),
                  pl.BlockSpec((D_MODEL, 2 * xq), full), pl.BlockSpec((1, LANES), full)],
        out_specs=[pl.BlockSpec((tm, xq), row), pl.BlockSpec((tm, xq), row)],
        out_shape=[jax.ShapeDtypeStruct((n, xq), MXU_DTYPE)] * 2,
        compiler_params=pltpu.CompilerParams(dimension_semantics=("parallel",)),
        name="memory_kv",
    )(mem2, lw["gmem"], lw["wmem"], lw["gxk"])


def _lane_mask(width, parts, i):
    lane = lax.broadcasted_iota(jnp.int32, (1, width), 1)
    w = width // parts
    return (lane >= i * w) & (lane < (i + 1) * w)


def _flash_kernel(q_ref, k_ref, v_ref, o_ref, qs_ref, m_ref, l_ref, acc_ref, *, nh, tq, tk, n_kv):
    kw = q_ref.shape[1]
    vw = v_ref.shape[1]
    q = q_ref[...]
    for i in range(nh):
        qs_ref[i * tq:(i + 1) * tq, :] = jnp.where(_lane_mask(kw, nh, i), q, jnp.zeros_like(q))
    m_ref[...] = jnp.full(m_ref.shape, -jnp.inf, jnp.float32)
    l_ref[...] = jnp.zeros(l_ref.shape, jnp.float32)
    acc_ref[...] = jnp.zeros(acc_ref.shape, jnp.float32)

    def body(j, carry):
        off = pl.multiple_of(j * tk, tk)
        k = k_ref[pl.ds(off, tk), :]
        v = v_ref[pl.ds(off, tk), :]
        s = _dot_nt(qs_ref[...], k)
        m_prev = m_ref[...]
        m_new = jnp.maximum(m_prev, jnp.max(s, axis=-1, keepdims=True))
        alpha = jnp.exp(m_prev - m_new)
        p = jnp.exp(s - m_new)
        l_ref[...] = alpha * l_ref[...] + jnp.sum(p, axis=-1, keepdims=True)
        acc_ref[...] = alpha * acc_ref[...] + _dot(p.astype(v.dtype), v)
        m_ref[...] = m_new
        return carry

    lax.fori_loop(0, n_kv, body, 0)
    o_stack = acc_ref[...] / l_ref[...]
    o = jnp.zeros((tq, vw), jnp.float32)
    for i in range(nh):
        o = o + jnp.where(_lane_mask(vw, nh, i), o_stack[i * tq:(i + 1) * tq, :], 0.0)
    o_ref[...] = o.astype(o_ref.dtype)


def _flash_attention(q, k, v, *, batch, seq, kv_len, nh, vw, tq, tk):
    kw = 256
    groups = q.shape[1] // kw
    tq = min(tq, seq)
    tk = min(tk, kv_len)
    nq = seq // tq
    m = nh * tq
    kern = functools.partial(_flash_kernel, nh=nh, tq=tq, tk=tk, n_kv=kv_len // tk)
    return pl.pallas_call(
        kern,
        grid=(batch, groups, nq),
        in_specs=[pl.BlockSpec((tq, kw), lambda b, g, i: (b * nq + i, g)),
                  pl.BlockSpec((kv_len, kw), lambda b, g, i: (b, g)),
                  pl.BlockSpec((kv_len, vw), lambda b, g, i: (b, g))],
        out_specs=pl.BlockSpec((tq, vw), lambda b, g, i: (b * nq + i, g)),
        out_shape=jax.ShapeDtypeStruct((batch * seq, groups * vw), MXU_DTYPE),
        scratch_shapes=[pltpu.VMEM((m, kw), MXU_DTYPE), pltpu.VMEM((m, 1), jnp.float32),
                        pltpu.VMEM((m, 1), jnp.float32), pltpu.VMEM((m, vw), jnp.float32)],
        compiler_params=pltpu.CompilerParams(dimension_semantics=("parallel", "parallel", "parallel"),
                                             vmem_limit_bytes=VMEM_LIMIT),
        name=f"flash_attention_h{nh}_v{vw}",
    )(q, k, v)


def _window_kernel(sink_ref, q_ref, k_ref, v_ref, o_ref, *, tq, seq):
    g = pl.program_id(1)
    j = pl.program_id(2)
    nh = 4
    win = tq + 2 * C_WINDOW
    start = jnp.clip(j * tq - C_WINDOW, 0, seq - win)
    start = pl.multiple_of(start, C_WINDOW)
    k = k_ref[pl.ds(start, win), :]
    v = v_ref[pl.ds(start, win), :]
    q = q_ref[...]
    kw = q.shape[1]
    qs = jnp.concatenate([jnp.where(_lane_mask(kw, nh, i), q, jnp.zeros_like(q)) for i in range(nh)], axis=0)
    s = _dot_nt(qs, k)
    tpos = j * tq + lax.broadcasted_iota(jnp.int32, (tq, win), 0)
    spos = start + lax.broadcasted_iota(jnp.int32, (tq, win), 1)
    rel = jnp.abs(spos - tpos)
    valid = rel <= C_WINDOW
    dist = rel.astype(jnp.float32)
    ps, rs = [], []
    for i in range(nh):
        slope = jnp.where(g == 0, 2.0 ** -(i + 1), 2.0 ** -(i + 5))
        si = jnp.where(valid, s[i * tq:(i + 1) * tq, :] - slope * dist, NEG_INF)
        sink = sink_ref[nh * g + i]
        m = jnp.maximum(jnp.max(si, axis=-1, keepdims=True), sink)
        p = jnp.exp(si - m)
        rs.append(1.0 / (jnp.sum(p, axis=-1, keepdims=True) + jnp.exp(sink - m)))
        ps.append(p.astype(v.dtype))
    pv = _dot(jnp.concatenate(ps, axis=0), v)
    o = jnp.zeros((tq, kw), jnp.float32)
    for i in range(nh):
        o = o + jnp.where(_lane_mask(kw, nh, i), pv[i * tq:(i + 1) * tq, :] * rs[i], 0.0)
    o_ref[...] = o.astype(o_ref.dtype)


def _window_attention(q, k, v, sink, *, batch, seq, tq):
    kw = 256
    groups = q.shape[1] // kw
    nq = seq // tq
    kern = functools.partial(_window_kernel, tq=tq, seq=seq)
    return pl.pallas_call(
        kern,
        grid=(batch, groups, nq),
        in_specs=[pl.BlockSpec(memory_space=pltpu.SMEM),
                  pl.BlockSpec((tq, kw), lambda b, g, i: (b * nq + i, g)),
                  pl.BlockSpec((seq, kw), lambda b, g, i: (b, g)),
                  pl.BlockSpec((seq, kw), lambda b, g, i: (b, g))],
        out_specs=pl.BlockSpec((tq, kw), lambda b, g, i: (b * nq + i, g)),
        out_shape=jax.ShapeDtypeStruct((batch * seq, groups * kw), MXU_DTYPE),
        compiler_params=pltpu.CompilerParams(dimension_semantics=("parallel", "parallel", "parallel"),
                                             vmem_limit_bytes=VMEM_LIMIT),
        name="window_attention",
    )(sink, q, k, v)


def _merge_kernel(x_ref, oa_ref, ob_ref, oc_ref, ox_ref, g1_ref, wg_ref, wbr_ref, wo_ref, g2_ref,
                  wrh_ref, wrl_ref, y_ref, h2_ref, aff_ref):
    x = x_ref[...]
    hb = _rms(x, g1_ref[...]).astype(MXU_DTYPE)
    merged = jnp.zeros(x.shape, jnp.float32)
    r0 = 0
    for i, o_ref in enumerate((oa_ref, ob_ref, oc_ref, ox_ref)):
        r1 = r0 + o_ref.shape[1]
        gate = jax.nn.sigmoid(_dot(hb, wg_ref[:, i * D_MODEL:(i + 1) * D_MODEL]))
        merged = merged + gate * _dot(o_ref[...], wbr_ref[r0:r1, :])
        r0 = r1
    y = x + _dot(merged.astype(MXU_DTYPE), wo_ref[...])
    y_ref[...] = y
    h2 = _rms(y, g2_ref[...])
    h2_hi = h2.astype(MXU_DTYPE)
    h2_lo = (h2 - h2_hi.astype(jnp.float32)).astype(MXU_DTYPE)
    h2_ref[...] = h2_hi
    wrh, wrl = wrh_ref[...], wrl_ref[...]
    logits = _dot_nt(wrh, h2_hi) + _dot_nt(wrh, h2_lo) + _dot_nt(wrl, h2_hi)
    e = jnp.exp(logits - jnp.max(logits, axis=0, keepdims=True))
    aff_ref[...] = e / jnp.sum(e, axis=0, keepdims=True)


def _merge(x2, oa, ob, oc, ox, lw, tm):
    n = x2.shape[0]
    row = lambda i: (i, 0)
    full = lambda i: (0, 0)
    return pl.pallas_call(
        _merge_kernel,
        grid=(n // tm,),
        in_specs=[pl.BlockSpec((tm, D_MODEL), row),
                  pl.BlockSpec((tm, oa.shape[1]), row), pl.BlockSpec((tm, ob.shape[1]), row),
                  pl.BlockSpec((tm, oc.shape[1]), row), pl.BlockSpec((tm, ox.shape[1]), row),
                  pl.BlockSpec((1, D_MODEL), full),
                  pl.BlockSpec(lw["wg"].shape, full), pl.BlockSpec(lw["wbr"].shape, full),
                  pl.BlockSpec(lw["wo"].shape, full), pl.BlockSpec((1, D_MODEL), full),
                  pl.BlockSpec(lw["wrh"].shape, full), pl.BlockSpec(lw["wrl"].shape, full)],
        out_specs=[pl.BlockSpec((tm, D_MODEL), row), pl.BlockSpec((tm, D_MODEL), row),
                   pl.BlockSpec((N_EXPERTS, tm), lambda i: (0, i))],
        out_shape=[jax.ShapeDtypeStruct((n, D_MODEL), jnp.float32),
                   jax.ShapeDtypeStruct((n, D_MODEL), MXU_DTYPE),
                   jax.ShapeDtypeStruct((N_EXPERTS, n), jnp.float32)],
        compiler_params=pltpu.CompilerParams(dimension_semantics=("parallel",), vmem_limit_bytes=VMEM_LIMIT),
        name="merge_router",
    )(x2, oa, ob, oc, ox, lw["g1"], lw["wg"], lw["wbr"], lw["wo"], lw["g2"], lw["wrh"], lw["wrl"])


def _expert_kernel(xs_ref, wgu_ref, wd_ref, gate_ref, y_ref):
    gu = _dot(xs_ref[...], wgu_ref[0])
    g, u = gu[:, :D_EXPERT], gu[:, D_EXPERT:]
    act = (g * jax.nn.sigmoid(g) * u).astype(MXU_DTYPE)
    y_ref[...] = _dot(act, wd_ref[0]) * gate_ref[...]


def _expert_ffn(xs, gate, wgu, wd, cap, tc):
    nt = cap // tc
    return pl.pallas_call(
        _expert_kernel,
        grid=(N_EXPERTS, nt),
        in_specs=[pl.BlockSpec((tc, D_MODEL), lambda e, c: (e * nt + c, 0)),
                  pl.BlockSpec((1, D_MODEL, 2 * D_EXPERT), lambda e, c: (e, 0, 0)),
                  pl.BlockSpec((1, D_EXPERT, D_MODEL), lambda e, c: (e, 0, 0)),
                  pl.BlockSpec((tc, 1), lambda e, c: (e * nt + c, 0))],
        out_specs=pl.BlockSpec((tc, D_MODEL), lambda e, c: (e * nt + c, 0)),
        out_shape=jax.ShapeDtypeStruct((N_EXPERTS * cap, D_MODEL), jnp.float32),
        compiler_params=pltpu.CompilerParams(dimension_semantics=("parallel", "parallel"),
                                             vmem_limit_bytes=VMEM_LIMIT),
        name="expert_ffn",
    )(xs, wgu, wd, gate)


def _rope_tables(seq):
    pos = jnp.arange(seq, dtype=jnp.int32)
    inv = jnp.power(ROPE_THETA, -jnp.arange(0, 32, 2, dtype=jnp.float32) / 32.0)
    ang_row = (pos // GRID_W).astype(jnp.float32)[:, None] * inv[None, :]
    ang_col = (pos % GRID_W).astype(jnp.float32)[:, None] * inv[None, :]
    ang_seq = pos.astype(jnp.float32)[:, None] * inv[None, :]
    sign = jnp.concatenate([-jnp.ones((16,), jnp.float32), jnp.ones((16,), jnp.float32)])

    def pair(ang):
        return jnp.tile(jnp.cos(ang), (1, 2)), jnp.tile(jnp.sin(ang), (1, 2)) * sign[None, :]

    cr, sr = pair(ang_row)
    cc, sc = pair(ang_col)
    cs, ss = pair(ang_seq)
    one = jnp.ones((seq, 32), jnp.float32)
    zero = jnp.zeros((seq, 32), jnp.float32)
    return {
        "cosa": jnp.concatenate([cr, cc, cr, cc], axis=1),
        "sina": jnp.concatenate([sr, sc, sr, sc], axis=1),
        "cosb": jnp.concatenate([one, one, cs, one], axis=1),
        "sinb": jnp.concatenate([zero, zero, ss, zero], axis=1),
    }


def _prep_layer(l, p):
    f32 = jnp.float32
    w_in = p["w_in"][l]
    a0, b0, c0, x0, g0 = 0, 768, 1440, 2208, 2464

    def dup2(w):
        return jnp.concatenate([w[:, :64], w[:, :64], w[:, 64:], w[:, 64:]], axis=1)

    kr = jnp.zeros((D_MODEL, LANES), f32).at[:, 64:96].set(w_in[:, b0 + 640:b0 + 672])
    w1 = jnp.concatenate([
        w_in[:, a0:a0 + 512], dup2(w_in[:, a0 + 512:a0 + 640]), dup2(w_in[:, a0 + 640:a0 + 768]),
        w_in[:, b0:b0 + 384], w_in[:, b0 + 384:b0 + 640],
        w_in[:, c0:c0 + 512], dup2(w_in[:, c0 + 512:c0 + 640]), dup2(w_in[:, c0 + 640:c0 + 768]),
        w_in[:, x0:x0 + 256], kr], axis=1)
    assert w1.shape[1] == _C_END

    wqb = jnp.zeros((B_Q_RANK, B_HEADS, LANES), f32).at[:, :, :B_QK].set(
        p["b_w_q_b"][l].reshape(B_Q_RANK, B_HEADS, B_QK)).reshape(B_Q_RANK, B_HEADS * LANES)
    wkv = p["b_w_kv_b"][l].reshape(B_KV_RANK, B_HEADS, 2 * B_NOPE)
    wkb = jnp.zeros((B_KV_RANK, B_HEADS, LANES), f32).at[:, :, :B_NOPE].set(
        wkv[:, :, :B_NOPE]).reshape(B_KV_RANK, B_HEADS * LANES)
    wvb = wkv[:, :, B_NOPE:].reshape(B_KV_RANK, B_HEADS * B_NOPE)

    def g64(g, scale=1.0):
        return jnp.tile(g * scale, 2)

    def g96(g, scale=1.0):
        return jnp.concatenate([g * scale, jnp.zeros((LANES - B_QK,), f32)])

    gains = jnp.stack([
        g64(p["a_q_norm"][l], HEAD ** -0.5), g64(p["a_k_norm"][l]),
        g96(p["b_q_norm"][l], B_QK ** -0.5), g96(p["b_k_norm"][l]),
        g64(p["c_q_norm"][l], HEAD ** -0.5), g64(p["c_k_norm"][l]),
        g64(p["x_q_norm"][l], HEAD ** -0.5), jnp.zeros((LANES,), f32)])

    wr = p["w_router"][l].T
    wrh = wr.astype(MXU_DTYPE)
    wrl = (wr - wrh.astype(f32)).astype(MXU_DTYPE)
    bf = lambda a: a.astype(MXU_DTYPE)
    return {
        "g1": p["norm1_g"][l][None, :], "w1": bf(w1), "wqb": bf(wqb), "wkb": bf(wkb), "wvb": bf(wvb),
        "gains": gains, "gbq": p["b_q_a_norm"][l][None, :], "gbkv": p["b_kv_a_norm"][l][None, :],
        "gmem": p["mem_norm_g"][l][None, :], "wmem": bf(p["w_mem_kv"][l]), "gxk": g64(p["x_k_norm"][l])[None, :],
        "sink": p["c_sink"][l],
        "wg": bf(w_in[:, g0:]), "wbr": bf(p["w_branch"][l]), "wo": bf(p["w_out"][l]),
        "g2": p["norm2_g"][l][None, :], "wrh": wrh, "wrl": wrl,
        "wgu": bf(p["w_gate_up"][l]), "wd": bf(p["w_down"][l]),
    }


def _trunk(x, mem, layers):
    batch, seq, _ = x.shape
    n = batch * seq
    tm = min(256, seq)
    cap = EC_FACTOR * n // N_EXPERTS
    tc = min(256, cap)
    tabs = _rope_tables(seq)
    x2 = x.reshape(n, D_MODEL)
    mem2 = mem.reshape(batch * N_MEM, D_MODEL)
    for lw in layers:
        qa, ka, va, qb, kb, vb, qc, kc, vc, qx = _in_projection(x2, lw, tabs, seq, tm)
        kx, vx = _memory_kv(mem2, lw)
        oa = _flash_attention(qa, ka, va, batch=batch, seq=seq, kv_len=seq, nh=4, vw=256, tq=256, tk=512)
        ob = _flash_attention(qb, kb, vb, batch=batch, seq=seq, kv_len=seq, nh=2, vw=128, tq=512, tk=512)
        oc = _window_attention(qc, kc, vc, lw["sink"], batch=batch, seq=seq, tq=min(256, seq - 2 * C_WINDOW))
        ox = _flash_attention(qx, kx, vx, batch=batch, seq=seq, kv_len=N_MEM, nh=4, vw=256, tq=256, tk=N_MEM)
        y, h2, aff = _merge(x2, oa, ob, oc, ox, lw, tm)
        gate, idx = lax.top_k(aff, cap)
        flat = idx.reshape(-1)
        xs = jnp.take(h2, flat, axis=0)
        ye = _expert_ffn(xs, gate.reshape(-1, 1), lw["wgu"], lw["wd"], cap, tc)
        x2 = y.at[flat].add(ye)
    return x2.reshape(batch, seq, D_MODEL)


def kernel(x_prompt, x_sample, mem_prompt, mem_sample, norm1_g, w_in, a_q_norm, a_k_norm, b_q_a_norm,
           b_w_q_b, b_kv_a_norm, b_w_kv_b, b_q_norm, b_k_norm, c_q_norm, c_k_norm, c_sink, x_q_norm,
           x_k_norm, mem_norm_g, w_mem_kv, w_branch, w_out, norm2_g, w_router, w_gate_up, w_down):
    p = dict(norm1_g=norm1_g, w_in=w_in, a_q_norm=a_q_norm, a_k_norm=a_k_norm, b_q_a_norm=b_q_a_norm,
             b_w_q_b=b_w_q_b, b_kv_a_norm=b_kv_a_norm, b_w_kv_b=b_w_kv_b, b_q_norm=b_q_norm, b_k_norm=b_k_norm,
             c_q_norm=c_q_norm, c_k_norm=c_k_norm, c_sink=c_sink, x_q_norm=x_q_norm, x_k_norm=x_k_norm,
             mem_norm_g=mem_norm_g, w_mem_kv=w_mem_kv, w_branch=w_branch, w_out=w_out, norm2_g=norm2_g,
             w_router=w_router, w_gate_up=w_gate_up, w_down=w_down)
    layers = [_prep_layer(l, p) for l in range(w_in.shape[0])]
    return (_trunk(x_prompt, mem_prompt, layers), _trunk(x_sample, mem_sample, layers))
```

```python
import functools

import jax
import jax.numpy as jnp
from jax import lax
from jax.experimental import pallas as pl
from jax.experimental.pallas import tpu as pltpu

D_MODEL = 1024
GRID_W = 64
N_MEM = 256
EPS = 1e-6
ROPE_THETA = 10000.0
NEG_INF = -1e30
HEAD = 64
A_KV_HEADS = 2
B_HEADS = 8
B_Q_RANK = 384
B_KV_RANK = 256
B_NOPE = 64
B_ROPE = 32
B_QK = B_NOPE + B_ROPE
C_WINDOW = 128
X_HEADS = 4
N_EXPERTS = 16
EC_FACTOR = 2
D_EXPERT = 1024
LANES = 128

MXU_DTYPE = jnp.bfloat16
VMEM_LIMIT = 56 * 1024 * 1024
LOG2E = 1.4426950408889634
TQ_A, TK_A = 256, 512
TQ_B, TK_B = 512, 512
TQ_X = 256

_C_AQ, _C_AK, _C_AV = 0, 512, 640
_C_BCQ, _C_BCKV = 768, 1152
_C_CQ, _C_CK, _C_CV = 1408, 1920, 2176
_C_XQ, _C_KR, _C_END = 2432, 2688, 2816
V_ROWS = 80


def _dot(a, b):
    return jnp.dot(a, b, preferred_element_type=jnp.float32)


def _dot_nt(a, b):
    return lax.dot_general(a, b, (((1,), (1,)), ((), ())), preferred_element_type=jnp.float32)


def _rms(x, g):
    return x * lax.rsqrt(jnp.mean(x * x, axis=-1, keepdims=True) + EPS) * g


def _head_norm(z, gain, head_w, n_valid):
    lane = lax.broadcasted_iota(jnp.int32, (1, LANES), 1)
    outs = []
    for c in range(z.shape[1] // LANES):
        blk = z[:, c * LANES:(c + 1) * LANES]
        sq = blk * blk
        if head_w == LANES:
            r = lax.rsqrt(jnp.sum(sq, axis=-1, keepdims=True) * (1.0 / n_valid) + EPS)
        else:
            lo = lane < head_w
            s_lo = jnp.sum(jnp.where(lo, sq, 0.0), axis=-1, keepdims=True)
            s_hi = jnp.sum(jnp.where(lo, 0.0, sq), axis=-1, keepdims=True)
            r = jnp.where(lo, lax.rsqrt(s_lo * (1.0 / n_valid) + EPS), lax.rsqrt(s_hi * (1.0 / n_valid) + EPS))
        outs.append(blk * r * gain)
    return outs


def _rope(blocks, cos, sin_signed):
    lane = lax.broadcasted_iota(jnp.int32, (1, LANES), 1)
    first = (lane % 32) < 16
    outs = []
    for blk in blocks:
        up = pltpu.roll(blk, LANES - 16, 1)
        dn = pltpu.roll(blk, 16, 1)
        outs.append(blk * cos + jnp.where(first, up, dn) * sin_signed)
    return outs


def _cat(blocks, dtype):
    return jnp.concatenate(blocks, axis=1).astype(dtype)


def _transposed_v(v, dtype):
    vt = v.T
    ones = jnp.ones((V_ROWS - HEAD, v.shape[0]), jnp.float32)
    parts = []
    for h in range(v.shape[1] // HEAD):
        parts += [vt[h * HEAD:(h + 1) * HEAD, :], ones]
    return jnp.concatenate(parts, axis=0).astype(dtype)


def _inproj_kernel(x_ref, g1_ref, w1_ref, wqb_ref, wkb_ref, wvb_ref, gains_ref, gbq_ref, gbkv_ref,
                   cosa_ref, sina_ref, cosb_ref, sinb_ref,
                   qa_ref, ka_ref, va_ref, qb_ref, kb_ref, vb_ref, qc_ref, kc_ref, vc_ref, qx_ref):
    x = x_ref[...]
    hb = _rms(x, g1_ref[...]).astype(MXU_DTYPE)
    z = _dot(hb, w1_ref[...])
    gains = gains_ref[...]
    cosa, sina = cosa_ref[...], sina_ref[...]
    cosb, sinb = cosb_ref[...], sinb_ref[...]
    od = qa_ref.dtype

    qa = _rope(_head_norm(z[:, _C_AQ:_C_AK], gains[0:1], HEAD, HEAD), cosa, sina)
    qa_ref[...] = _cat(qa, od)
    ka = _rope(_head_norm(z[:, _C_AK:_C_AV], gains[1:2], HEAD, HEAD), cosa, sina)
    ka_ref[...] = ka[0].astype(od)
    va_ref[...] = _transposed_v(z[:, _C_AV:_C_BCQ], od)

    cq = _rms(z[:, _C_BCQ:_C_BCKV], gbq_ref[...]).astype(MXU_DTYPE)
    ckv = _rms(z[:, _C_BCKV:_C_CQ], gbkv_ref[...]).astype(MXU_DTYPE)
    q8 = _dot(cq, wqb_ref[...])
    k8 = _dot(ckv, wkb_ref[...])
    kr = z[:, _C_KR:_C_END]
    k8 = k8 + jnp.concatenate([kr] * B_HEADS, axis=1)
    qb_ref[...] = _cat(_rope(_head_norm(q8, gains[2:3], LANES, B_QK), cosb, sinb), od)
    kb_ref[...] = _cat(_rope(_head_norm(k8, gains[3:4], LANES, B_QK), cosb, sinb), od)
    vb_ref[...] = _transposed_v(_dot(ckv, wvb_ref[...]), od)

    qc_ref[...] = _cat(_head_norm(z[:, _C_CQ:_C_CK], gains[4:5], HEAD, HEAD), od)
    kc = _head_norm(z[:, _C_CK:_C_CV], gains[5:6], HEAD, HEAD)
    kc_ref[...] = _cat([kc[0], kc[0], kc[1], kc[1]], od)
    vc = z[:, _C_CV:_C_XQ]
    vc_ref[...] = _cat([vc[:, :LANES], vc[:, :LANES], vc[:, LANES:], vc[:, LANES:]], od)

    qx_ref[...] = _cat(_head_norm(z[:, _C_XQ:_C_KR], gains[6:7], HEAD, HEAD), od)


def _in_projection(x2, lw, tabs, seq, tm):
    n = x2.shape[0]
    nt = seq // tm
    row = lambda i: (i, 0)
    full = lambda i: (0, 0)
    tab = lambda i: (i % nt, 0)
    col = lambda i: (0, i)
    outs_def = [(512, False), (LANES, False), (A_KV_HEADS * V_ROWS, True), (1024, False), (1024, False),
                (B_HEADS * V_ROWS, True), (512, False), (512, False), (512, False), (256, False)]
    out_specs = [pl.BlockSpec((w, tm), col) if t else pl.BlockSpec((tm, w), row) for w, t in outs_def]
    out_shape = [jax.ShapeDtypeStruct((w, n) if t else (n, w), MXU_DTYPE) for w, t in outs_def]
    outs = pl.pallas_call(
        _inproj_kernel,
        grid=(n // tm,),
        in_specs=[
            pl.BlockSpec((tm, D_MODEL), row),
            pl.BlockSpec((1, D_MODEL), full),
            pl.BlockSpec(lw["w1"].shape, full),
            pl.BlockSpec(lw["wqb"].shape, full),
            pl.BlockSpec(lw["wkb"].shape, full),
            pl.BlockSpec(lw["wvb"].shape, full),
            pl.BlockSpec(lw["gains"].shape, full),
            pl.BlockSpec((1, B_Q_RANK), full),
            pl.BlockSpec((1, B_KV_RANK), full),
            pl.BlockSpec((tm, LANES), tab),
            pl.BlockSpec((tm, LANES), tab),
            pl.BlockSpec((tm, LANES), tab),
            pl.BlockSpec((tm, LANES), tab),
        ],
        out_specs=out_specs,
        out_shape=out_shape,
        compiler_params=pltpu.CompilerParams(dimension_semantics=("parallel",), vmem_limit_bytes=VMEM_LIMIT),
        name="in_projection",
    )(x2, lw["g1"], lw["w1"], lw["wqb"], lw["wkb"], lw["wvb"], lw["gains"], lw["gbq"], lw["gbkv"],
      tabs["cosa"], tabs["sina"], tabs["cosb"], tabs["sinb"])
    return outs


def _memkv_kernel(mem_ref, g_ref, w_ref, gain_ref, k_ref, v_ref):
    m = _rms(mem_ref[...], g_ref[...]).astype(MXU_DTYPE)
    kv = _dot(m, w_ref[...])
    half = kv.shape[1] // 2
    k_ref[...] = _cat(_head_norm(kv[:, :half], gain_ref[...], HEAD, HEAD), k_ref.dtype)
    v_ref[...] = _transposed_v(kv[:, half:], v_ref.dtype)


def _memory_kv(mem2, lw):
    n = mem2.shape[0]
    tm = N_MEM
    xq = X_HEADS * HEAD
    row = lambda i: (i, 0)
    full = lambda i: (0, 0)
    return pl.pallas_call(
        _memkv_kernel,
        grid=(n // tm,),
        in_specs=[pl.BlockSpec((tm, D_MODEL), row), pl.BlockSpec((1, D_MODEL), full),
                  pl.BlockSpec((D_MODEL, 2 * xq), full), pl.BlockSpec((1, LANES), full)],
        out_specs=[pl.BlockSpec((tm, xq), row), pl.BlockSpec((X_HEADS * V_ROWS, tm), lambda i: (0, i))],
        out_shape=[jax.ShapeDtypeStruct((n, xq), MXU_DTYPE),
                   jax.ShapeDtypeStruct((X_HEADS * V_ROWS, n), MXU_DTYPE)],
        compiler_params=pltpu.CompilerParams(dimension_semantics=("parallel",)),
        name="memory_kv",
    )(mem2, lw["gmem"], lw["wmem"], lw["gxk"])


def _lane_mask(width, parts, i):
    lane = lax.broadcasted_iota(jnp.int32, (1, width), 1)
    w = width // parts
    return (lane >= i * w) & (lane < (i + 1) * w)


def _flash_kernel(q_ref, k_ref, v_ref, o_ref, qs_ref, s_ref, m_ref, acc_ref, *, mode, nh, tq, tk, n_kv):
    g = pl.program_id(1)
    qt = q_ref[...].astype(jnp.float32).T
    kdim = qs_ref.shape[0]
    for i in range(nh):
        if mode == "gqa":
            blk = qt[i * HEAD:(i + 1) * HEAD, :]
            half = lax.broadcasted_iota(jnp.int32, (kdim, tq), 0) // HEAD
            op = jnp.where(half == g, jnp.concatenate([blk, blk], axis=0), 0.0)
        elif mode == "mha":
            op = qt[i * LANES:(i + 1) * LANES, :]
        else:
            head = lax.broadcasted_iota(jnp.int32, (kdim, tq), 0) // HEAD
            op = jnp.where(head == i, qt, 0.0)
        qs_ref[:, i * tq:(i + 1) * tq] = op.astype(qs_ref.dtype)
    m_ref[...] = jnp.full(m_ref.shape, -jnp.inf, jnp.float32)
    acc_ref[...] = jnp.zeros(acc_ref.shape, jnp.float32)

    def score(j, slot):
        off = pl.multiple_of(j * tk, tk)
        if mode == "mha":
            for i in range(nh):
                cols = slice(i * tq, (i + 1) * tq)
                s_ref[slot, :, cols] = _dot(k_ref[pl.ds(off, tk), i * LANES:(i + 1) * LANES], qs_ref[:, cols])
        else:
            s_ref[slot] = _dot(k_ref[pl.ds(off, tk), :], qs_ref[...])

    def consume(j, slot):
        off = pl.multiple_of(j * tk, tk)
        s = s_ref[slot]
        m_prev = m_ref[0:1, :]
        m_new = jnp.maximum(m_prev, jnp.max(s, axis=0, keepdims=True))
        alpha = jnp.exp2(m_prev - m_new)
        p = jnp.exp2(s - m_new).astype(v_ref.dtype)
        if mode == "gqa":
            pv = _dot(v_ref[:, pl.ds(off, tk)], p)
        else:
            pv = jnp.concatenate(
                [_dot(v_ref[i * V_ROWS:(i + 1) * V_ROWS, pl.ds(off, tk)], p[:, i * tq:(i + 1) * tq])
                 for i in range(nh)], axis=1)
        acc_ref[...] = alpha * acc_ref[...] + pv
        m_ref[0:1, :] = m_new

    score(0, 0)
    if n_kv > 1:
        def pair(jj, carry):
            score(2 * jj + 1, 1)
            consume(2 * jj, 0)
            score(2 * jj + 2, 0)
            consume(2 * jj + 1, 1)
            return carry

        lax.fori_loop(0, n_kv // 2 - 1, pair, 0)
        score(n_kv - 1, 1)
        consume(n_kv - 2, 0)
        consume(n_kv - 1, 1)
    else:
        consume(0, 0)
    a = acc_ref[...]
    ot = a[:HEAD, :] / a[HEAD:HEAD + 1, :]
    o_ref[...] = jnp.concatenate([ot[:, i * tq:(i + 1) * tq] for i in range(nh)], axis=0).T.astype(o_ref.dtype)


def _flash_attention(q, k, vt, *, mode, batch, seq, kv_len, tq, tk):
    kw = 256
    groups = q.shape[1] // kw
    nh = {"gqa": 4, "mha": 2, "cross": 4}[mode]
    kcols = {"gqa": LANES, "mha": kw, "cross": kw}[mode]
    kdim = {"gqa": LANES, "mha": LANES, "cross": kw}[mode]
    vrows = V_ROWS if mode == "gqa" else nh * V_ROWS
    k_map = (lambda b, g, i: (b, 0)) if mode == "gqa" else (lambda b, g, i: (b, g))
    tq = min(tq, seq)
    tk = min(tk, kv_len)
    nq = seq // tq
    ow = nh * HEAD
    kern = functools.partial(_flash_kernel, mode=mode, nh=nh, tq=tq, tk=tk, n_kv=kv_len // tk)
    return pl.pallas_call(
        kern,
        grid=(batch, groups, nq),
        in_specs=[pl.BlockSpec((tq, kw), lambda b, g, i: (b * nq + i, g)),
                  pl.BlockSpec((kv_len, kcols), k_map),
                  pl.BlockSpec((vrows, kv_len), lambda b, g, i: (g, b))],
        out_specs=pl.BlockSpec((tq, ow), lambda b, g, i: (b * nq + i, g)),
        out_shape=jax.ShapeDtypeStruct((batch * seq, groups * ow), MXU_DTYPE),
        scratch_shapes=[pltpu.VMEM((kdim, nh * tq), MXU_DTYPE), pltpu.VMEM((2, tk, nh * tq), jnp.float32),
                        pltpu.VMEM((8, nh * tq), jnp.float32), pltpu.VMEM((V_ROWS, nh * tq), jnp.float32)],
        compiler_params=pltpu.CompilerParams(dimension_semantics=("parallel", "parallel", "parallel"),
                                             vmem_limit_bytes=VMEM_LIMIT),
        name=f"flash_attention_{mode}",
    )(q, k, vt)


def _window_kernel(sink_ref, q_ref, k_ref, v_ref, o_ref, *, tq, seq):
    g = pl.program_id(1)
    j = pl.program_id(2)
    nh = 4
    win = tq + 2 * C_WINDOW
    start = jnp.clip(j * tq - C_WINDOW, 0, seq - win)
    start = pl.multiple_of(start, C_WINDOW)
    k = k_ref[pl.ds(start, win), :]
    v = v_ref[pl.ds(start, win), :]
    q = q_ref[...]
    kw = q.shape[1]
    qs = jnp.concatenate([jnp.where(_lane_mask(kw, nh, i), q, jnp.zeros_like(q)) for i in range(nh)], axis=0)
    s = _dot_nt(qs, k)
    tpos = j * tq + lax.broadcasted_iota(jnp.int32, (tq, win), 0)
    spos = start + lax.broadcasted_iota(jnp.int32, (tq, win), 1)
    rel = jnp.abs(spos - tpos)
    valid = rel <= C_WINDOW
    dist = rel.astype(jnp.float32)
    ps, rs = [], []
    for i in range(nh):
        slope = jnp.where(g == 0, 2.0 ** -(i + 1), 2.0 ** -(i + 5))
        si = jnp.where(valid, s[i * tq:(i + 1) * tq, :] - slope * dist, NEG_INF)
        sink = sink_ref[nh * g + i]
        m = jnp.maximum(jnp.max(si, axis=-1, keepdims=True), sink)
        p = jnp.exp(si - m)
        rs.append(1.0 / (jnp.sum(p, axis=-1, keepdims=True) + jnp.exp(sink - m)))
        ps.append(p.astype(v.dtype))
    pv = _dot(jnp.concatenate(ps, axis=0), v)
    o = jnp.zeros((tq, kw), jnp.float32)
    for i in range(nh):
        o = o + jnp.where(_lane_mask(kw, nh, i), pv[i * tq:(i + 1) * tq, :] * rs[i], 0.0)
    o_ref[...] = o.astype(o_ref.dtype)


def _window_attention(q, k, v, sink, *, batch, seq, tq):
    kw = 256
    groups = q.shape[1] // kw
    nq = seq // tq
    kern = functools.partial(_window_kernel, tq=tq, seq=seq)
    return pl.pallas_call(
        kern,
        grid=(batch, groups, nq),
        in_specs=[pl.BlockSpec(memory_space=pltpu.SMEM),
                  pl.BlockSpec((tq, kw), lambda b, g, i: (b * nq + i, g)),
                  pl.BlockSpec((seq, kw), lambda b, g, i: (b, g)),
                  pl.BlockSpec((seq, kw), lambda b, g, i: (b, g))],
        out_specs=pl.BlockSpec((tq, kw), lambda b, g, i: (b * nq + i, g)),
        out_shape=jax.ShapeDtypeStruct((batch * seq, groups * kw), MXU_DTYPE),
        compiler_params=pltpu.CompilerParams(dimension_semantics=("parallel", "parallel", "parallel"),
                                             vmem_limit_bytes=VMEM_LIMIT),
        name="window_attention",
    )(sink, q, k, v)


def _merge_kernel(x_ref, oa_ref, ob_ref, oc_ref, ox_ref, g1_ref, wg_ref, wbr_ref, wo_ref, g2_ref,
                  wrh_ref, wrl_ref, y_ref, h2_ref, aff_ref):
    x = x_ref[...]
    hb = _rms(x, g1_ref[...]).astype(MXU_DTYPE)
    merged = jnp.zeros(x.shape, jnp.float32)
    r0 = 0
    for i, o_ref in enumerate((oa_ref, ob_ref, oc_ref, ox_ref)):
        r1 = r0 + o_ref.shape[1]
        gate = jax.nn.sigmoid(_dot(hb, wg_ref[:, i * D_MODEL:(i + 1) * D_MODEL]))
        merged = merged + gate * _dot(o_ref[...], wbr_ref[r0:r1, :])
        r0 = r1
    y = x + _dot(merged.astype(MXU_DTYPE), wo_ref[...])
    y_ref[...] = y
    h2 = _rms(y, g2_ref[...])
    h2_hi = h2.astype(MXU_DTYPE)
    h2_lo = (h2 - h2_hi.astype(jnp.float32)).astype(MXU_DTYPE)
    h2_ref[...] = h2_hi
    wrh, wrl = wrh_ref[...], wrl_ref[...]
    logits = _dot_nt(wrh, h2_hi) + _dot_nt(wrh, h2_lo) + _dot_nt(wrl, h2_hi)
    e = jnp.exp(logits - jnp.max(logits, axis=0, keepdims=True))
    aff_ref[...] = e / jnp.sum(e, axis=0, keepdims=True)


def _merge(x2, oa, ob, oc, ox, lw, tm):
    n = x2.shape[0]
    row = lambda i: (i, 0)
    full = lambda i: (0, 0)
    return pl.pallas_call(
        _merge_kernel,
        grid=(n // tm,),
        in_specs=[pl.BlockSpec((tm, D_MODEL), row),
                  pl.BlockSpec((tm, oa.shape[1]), row), pl.BlockSpec((tm, ob.shape[1]), row),
                  pl.BlockSpec((tm, oc.shape[1]), row), pl.BlockSpec((tm, ox.shape[1]), row),
                  pl.BlockSpec((1, D_MODEL), full),
                  pl.BlockSpec(lw["wg"].shape, full), pl.BlockSpec(lw["wbr"].shape, full),
                  pl.BlockSpec(lw["wo"].shape, full), pl.BlockSpec((1, D_MODEL), full),
                  pl.BlockSpec(lw["wrh"].shape, full), pl.BlockSpec(lw["wrl"].shape, full)],
        out_specs=[pl.BlockSpec((tm, D_MODEL), row), pl.BlockSpec((tm, D_MODEL), row),
                   pl.BlockSpec((N_EXPERTS, tm), lambda i: (0, i))],
        out_shape=[jax.ShapeDtypeStruct((n, D_MODEL), jnp.float32),
                   jax.ShapeDtypeStruct((n, D_MODEL), MXU_DTYPE),
                   jax.ShapeDtypeStruct((N_EXPERTS, n), jnp.float32)],
        compiler_params=pltpu.CompilerParams(dimension_semantics=("parallel",), vmem_limit_bytes=VMEM_LIMIT),
        name="merge_router",
    )(x2, oa, ob, oc, ox, lw["g1"], lw["wg"], lw["wbr"], lw["wo"], lw["g2"], lw["wrh"], lw["wrl"])


def _expert_kernel(xs_ref, wgu_ref, wd_ref, gate_ref, y_ref):
    gu = _dot(xs_ref[...], wgu_ref[0])
    g, u = gu[:, :D_EXPERT], gu[:, D_EXPERT:]
    act = (g * jax.nn.sigmoid(g) * u).astype(MXU_DTYPE)
    y_ref[...] = _dot(act, wd_ref[0]) * gate_ref[...]


def _expert_ffn(xs, gate, wgu, wd, cap, tc):
    nt = cap // tc
    return pl.pallas_call(
        _expert_kernel,
        grid=(N_EXPERTS, nt),
        in_specs=[pl.BlockSpec((tc, D_MODEL), lambda e, c: (e * nt + c, 0)),
                  pl.BlockSpec((1, D_MODEL, 2 * D_EXPERT), lambda e, c: (e, 0, 0)),
                  pl.BlockSpec((1, D_EXPERT, D_MODEL), lambda e, c: (e, 0, 0)),
                  pl.BlockSpec((tc, 1), lambda e, c: (e * nt + c, 0))],
        out_specs=pl.BlockSpec((tc, D_MODEL), lambda e, c: (e * nt + c, 0)),
        out_shape=jax.ShapeDtypeStruct((N_EXPERTS * cap, D_MODEL), jnp.float32),
        compiler_params=pltpu.CompilerParams(dimension_semantics=("parallel", "parallel"),
                                             vmem_limit_bytes=VMEM_LIMIT),
        name="expert_ffn",
    )(xs, wgu, wd, gate)


def _rope_tables(seq):
    pos = jnp.arange(seq, dtype=jnp.int32)
    inv = jnp.power(ROPE_THETA, -jnp.arange(0, 32, 2, dtype=jnp.float32) / 32.0)
    ang_row = (pos // GRID_W).astype(jnp.float32)[:, None] * inv[None, :]
    ang_col = (pos % GRID_W).astype(jnp.float32)[:, None] * inv[None, :]
    ang_seq = pos.astype(jnp.float32)[:, None] * inv[None, :]
    sign = jnp.concatenate([-jnp.ones((16,), jnp.float32), jnp.ones((16,), jnp.float32)])

    def pair(ang):
        return jnp.tile(jnp.cos(ang), (1, 2)), jnp.tile(jnp.sin(ang), (1, 2)) * sign[None, :]

    cr, sr = pair(ang_row)
    cc, sc = pair(ang_col)
    cs, ss = pair(ang_seq)
    one = jnp.ones((seq, 32), jnp.float32)
    zero = jnp.zeros((seq, 32), jnp.float32)
    return {
        "cosa": jnp.concatenate([cr, cc, cr, cc], axis=1),
        "sina": jnp.concatenate([sr, sc, sr, sc], axis=1),
        "cosb": jnp.concatenate([one, one, cs, one], axis=1),
        "sinb": jnp.concatenate([zero, zero, ss, zero], axis=1),
    }


def _prep_layer(l, p):
    f32 = jnp.float32
    w_in = p["w_in"][l]
    a0, b0, c0, x0, g0 = 0, 768, 1440, 2208, 2464

    def dup2(w):
        return jnp.concatenate([w[:, :64], w[:, :64], w[:, 64:], w[:, 64:]], axis=1)

    kr = jnp.zeros((D_MODEL, LANES), f32).at[:, 64:96].set(w_in[:, b0 + 640:b0 + 672])
    w1 = jnp.concatenate([
        w_in[:, a0:a0 + 768],
        w_in[:, b0:b0 + 384], w_in[:, b0 + 384:b0 + 640],
        w_in[:, c0:c0 + 512], dup2(w_in[:, c0 + 512:c0 + 640]), dup2(w_in[:, c0 + 640:c0 + 768]),
        w_in[:, x0:x0 + 256], kr], axis=1)
    assert w1.shape[1] == _C_END

    wqb = jnp.zeros((B_Q_RANK, B_HEADS, LANES), f32).at[:, :, :B_QK].set(
        p["b_w_q_b"][l].reshape(B_Q_RANK, B_HEADS, B_QK)).reshape(B_Q_RANK, B_HEADS * LANES)
    wkv = p["b_w_kv_b"][l].reshape(B_KV_RANK, B_HEADS, 2 * B_NOPE)
    wkb = jnp.zeros((B_KV_RANK, B_HEADS, LANES), f32).at[:, :, :B_NOPE].set(
        wkv[:, :, :B_NOPE]).reshape(B_KV_RANK, B_HEADS * LANES)
    wvb = wkv[:, :, B_NOPE:].reshape(B_KV_RANK, B_HEADS * B_NOPE)

    def g64(g, scale=1.0):
        return jnp.tile(g * scale, 2)

    def g96(g, scale=1.0):
        return jnp.concatenate([g * scale, jnp.zeros((LANES - B_QK,), f32)])

    gains = jnp.stack([
        g64(p["a_q_norm"][l], HEAD ** -0.5 * LOG2E), g64(p["a_k_norm"][l]),
        g96(p["b_q_norm"][l], B_QK ** -0.5 * LOG2E), g96(p["b_k_norm"][l]),
        g64(p["c_q_norm"][l], HEAD ** -0.5), g64(p["c_k_norm"][l]),
        g64(p["x_q_norm"][l], HEAD ** -0.5 * LOG2E), jnp.zeros((LANES,), f32)])

    wr = p["w_router"][l].T
    wrh = wr.astype(MXU_DTYPE)
    wrl = (wr - wrh.astype(f32)).astype(MXU_DTYPE)
    bf = lambda a: a.astype(MXU_DTYPE)
    return {
        "g1": p["norm1_g"][l][None, :], "w1": bf(w1), "wqb": bf(wqb), "wkb": bf(wkb), "wvb": bf(wvb),
        "gains": gains, "gbq": p["b_q_a_norm"][l][None, :], "gbkv": p["b_kv_a_norm"][l][None, :],
        "gmem": p["mem_norm_g"][l][None, :], "wmem": bf(p["w_mem_kv"][l]), "gxk": g64(p["x_k_norm"][l])[None, :],
        "sink": p["c_sink"][l],
        "wg": bf(w_in[:, g0:]), "wbr": bf(p["w_branch"][l]), "wo": bf(p["w_out"][l]),
        "g2": p["norm2_g"][l][None, :], "wrh": wrh, "wrl": wrl,
        "wgu": bf(p["w_gate_up"][l]), "wd": bf(p["w_down"][l]),
    }


def _trunk(x, mem, layers):
    batch, seq, _ = x.shape
    n = batch * seq
    tm = min(256, seq)
    cap = EC_FACTOR * n // N_EXPERTS
    tc = min(256, cap)
    tabs = _rope_tables(seq)
    x2 = x.reshape(n, D_MODEL)
    mem2 = mem.reshape(batch * N_MEM, D_MODEL)
    for lw in layers:
        qa, ka, va, qb, kb, vb, qc, kc, vc, qx = _in_projection(x2, lw, tabs, seq, tm)
        kx, vx = _memory_kv(mem2, lw)
        oa = _flash_attention(qa, ka, va, mode="gqa", batch=batch, seq=seq, kv_len=seq, tq=TQ_A, tk=TK_A)
        ob = _flash_attention(qb, kb, vb, mode="mha", batch=batch, seq=seq, kv_len=seq, tq=TQ_B, tk=TK_B)
        oc = _window_attention(qc, kc, vc, lw["sink"], batch=batch, seq=seq, tq=min(256, seq - 2 * C_WINDOW))
        ox = _flash_attention(qx, kx, vx, mode="cross", batch=batch, seq=seq, kv_len=N_MEM, tq=TQ_X, tk=N_MEM)
        y, h2, aff = _merge(x2, oa, ob, oc, ox, lw, tm)
        gate, idx = lax.top_k(aff, cap)
        flat = idx.reshape(-1)
        xs = jnp.take(h2, flat, axis=0)
        ye = _expert_ffn(xs, gate.reshape(-1, 1), lw["wgu"], lw["wd"], cap, tc)
        x2 = y.at[flat].add(ye)
    return x2.reshape(batch, seq, D_MODEL)


def kernel(x_prompt, x_sample, mem_prompt, mem_sample, norm1_g, w_in, a_q_norm, a_k_norm, b_q_a_norm,
           b_w_q_b, b_kv_a_norm, b_w_kv_b, b_q_norm, b_k_norm, c_q_norm, c_k_norm, c_sink, x_q_norm,
           x_k_norm, mem_norm_g, w_mem_kv, w_branch, w_out, norm2_g, w_router, w_gate_up, w_down):
    p = dict(norm1_g=norm1_g, w_in=w_in, a_q_norm=a_q_norm, a_k_norm=a_k_norm, b_q_a_norm=b_q_a_norm,
             b_w_q_b=b_w_q_b, b_kv_a_norm=b_kv_a_norm, b_w_kv_b=b_w_kv_b, b_q_norm=b_q_norm, b_k_norm=b_k_norm,
             c_q_norm=c_q_norm, c_k_norm=c_k_norm, c_sink=c_sink, x_q_norm=x_q_norm, x_k_norm=x_k_norm,
             mem_norm_g=mem_norm_g, w_mem_kv=w_mem_kv, w_branch=w_branch, w_out=w_out, norm2_g=norm2_g,
             w_router=w_router, w_gate_up=w_gate_up, w_down=w_down)
    layers = [_prep_layer(l, p) for l in range(w_in.shape[0])]
    return (_trunk(x_prompt, mem_prompt, layers), _trunk(x_sample, mem_sample, layers))
```

```python
import functools

import jax
import jax.numpy as jnp
from jax import lax
from jax.experimental import pallas as pl
from jax.experimental.pallas import tpu as pltpu

D_MODEL = 1024
GRID_W = 64
N_MEM = 256
EPS = 1e-6
ROPE_THETA = 10000.0
NEG_INF = -1e30
HEAD = 64
A_KV_HEADS = 2
B_HEADS = 8
B_Q_RANK = 384
B_KV_RANK = 256
B_NOPE = 64
B_ROPE = 32
B_QK = B_NOPE + B_ROPE
C_WINDOW = 128
X_HEADS = 4
N_EXPERTS = 16
EC_FACTOR = 2
D_EXPERT = 1024
LANES = 128

MXU_DTYPE = jnp.bfloat16
VMEM_LIMIT = 56 * 1024 * 1024
LOG2E = 1.4426950408889634
TQ_A, TK_A = 256, 512
TQ_B, TK_B = 512, 512
TQ_X = 256

_C_AQ, _C_AK, _C_AV = 0, 512, 640
_C_BCQ, _C_BCKV = 768, 1152
_C_CQ, _C_CK, _C_CV = 1408, 1920, 2176
_C_XQ, _C_KR, _C_END = 2432, 2688, 2816
V_ROWS = 80


def _dot(a, b):
    return jnp.dot(a, b, preferred_element_type=jnp.float32)


def _dot_nt(a, b):
    return lax.dot_general(a, b, (((1,), (1,)), ((), ())), preferred_element_type=jnp.float32)


def _rms(x, g):
    return x * lax.rsqrt(jnp.mean(x * x, axis=-1, keepdims=True) + EPS) * g


def _head_norm(z, gain, head_w, n_valid):
    lane = lax.broadcasted_iota(jnp.int32, (1, LANES), 1)
    outs = []
    for c in range(z.shape[1] // LANES):
        blk = z[:, c * LANES:(c + 1) * LANES]
        sq = blk * blk
        if head_w == LANES:
            r = lax.rsqrt(jnp.sum(sq, axis=-1, keepdims=True) * (1.0 / n_valid) + EPS)
        else:
            lo = lane < head_w
            s_lo = jnp.sum(jnp.where(lo, sq, 0.0), axis=-1, keepdims=True)
            s_hi = jnp.sum(jnp.where(lo, 0.0, sq), axis=-1, keepdims=True)
            r = jnp.where(lo, lax.rsqrt(s_lo * (1.0 / n_valid) + EPS), lax.rsqrt(s_hi * (1.0 / n_valid) + EPS))
        outs.append(blk * r * gain)
    return outs


def _rope(blocks, cos, sin_signed):
    lane = lax.broadcasted_iota(jnp.int32, (1, LANES), 1)
    first = (lane % 32) < 16
    outs = []
    for blk in blocks:
        up = pltpu.roll(blk, LANES - 16, 1)
        dn = pltpu.roll(blk, 16, 1)
        outs.append(blk * cos + jnp.where(first, up, dn) * sin_signed)
    return outs


def _cat(blocks, dtype):
    return jnp.concatenate(blocks, axis=1).astype(dtype)


def _transposed_v(v, dtype):
    vt = v.T
    ones = jnp.ones((V_ROWS - HEAD, v.shape[0]), jnp.float32)
    parts = []
    for h in range(v.shape[1] // HEAD):
        parts += [vt[h * HEAD:(h + 1) * HEAD, :], ones]
    return jnp.concatenate(parts, axis=0).astype(dtype)


def _inproj_kernel(x_ref, g1_ref, w1_ref, wqb_ref, wkb_ref, wvb_ref, gains_ref, gbq_ref, gbkv_ref,
                   cosa_ref, sina_ref, cosb_ref, sinb_ref,
                   qa_ref, ka_ref, va_ref, qb_ref, kb_ref, vb_ref, qc_ref, kc_ref, vc_ref, qx_ref):
    x = x_ref[...]
    hb = _rms(x, g1_ref[...]).astype(MXU_DTYPE)
    z = _dot(hb, w1_ref[...])
    gains = gains_ref[...]
    cosa, sina = cosa_ref[...], sina_ref[...]
    cosb, sinb = cosb_ref[...], sinb_ref[...]
    od = qa_ref.dtype

    qa = _rope(_head_norm(z[:, _C_AQ:_C_AK], gains[0:1], HEAD, HEAD), cosa, sina)
    qa_ref[...] = _cat(qa, od)
    ka = _rope(_head_norm(z[:, _C_AK:_C_AV], gains[1:2], HEAD, HEAD), cosa, sina)
    ka_ref[...] = ka[0].astype(od)
    va_ref[...] = _transposed_v(z[:, _C_AV:_C_BCQ], od)

    cq = _rms(z[:, _C_BCQ:_C_BCKV], gbq_ref[...]).astype(MXU_DTYPE)
    ckv = _rms(z[:, _C_BCKV:_C_CQ], gbkv_ref[...]).astype(MXU_DTYPE)
    q8 = _dot(cq, wqb_ref[...])
    k8 = _dot(ckv, wkb_ref[...])
    kr = z[:, _C_KR:_C_END]
    k8 = k8 + jnp.concatenate([kr] * B_HEADS, axis=1)
    qb_ref[...] = _cat(_rope(_head_norm(q8, gains[2:3], LANES, B_QK), cosb, sinb), od)
    kb_ref[...] = _cat(_rope(_head_norm(k8, gains[3:4], LANES, B_QK), cosb, sinb), od)
    vb_ref[...] = _transposed_v(_dot(ckv, wvb_ref[...]), od)

    qc_ref[...] = _cat(_head_norm(z[:, _C_CQ:_C_CK], gains[4:5], HEAD, HEAD), od)
    kc = _head_norm(z[:, _C_CK:_C_CV], gains[5:6], HEAD, HEAD)
    kc_ref[...] = _cat([kc[0], kc[0], kc[1], kc[1]], od)
    vc = z[:, _C_CV:_C_XQ]
    vc_ref[...] = _cat([vc[:, :LANES], vc[:, :LANES], vc[:, LANES:], vc[:, LANES:]], od)

    qx_ref[...] = _cat(_head_norm(z[:, _C_XQ:_C_KR], gains[6:7], HEAD, HEAD), od)


def _in_projection(x2, lw, tabs, seq, tm):
    n = x2.shape[0]
    nt = seq // tm
    row = lambda i: (i, 0)
    full = lambda i: (0, 0)
    tab = lambda i: (i % nt, 0)
    col = lambda i: (0, i)
    outs_def = [(512, False), (LANES, False), (A_KV_HEADS * V_ROWS, True), (1024, False), (1024, False),
                (B_HEADS * V_ROWS, True), (512, False), (512, False), (512, False), (256, False)]
    out_specs = [pl.BlockSpec((w, tm), col) if t else pl.BlockSpec((tm, w), row) for w, t in outs_def]
    out_shape = [jax.ShapeDtypeStruct((w, n) if t else (n, w), MXU_DTYPE) for w, t in outs_def]
    outs = pl.pallas_call(
        _inproj_kernel,
        grid=(n // tm,),
        in_specs=[
            pl.BlockSpec((tm, D_MODEL), row),
            pl.BlockSpec((1, D_MODEL), full),
            pl.BlockSpec(lw["w1"].shape, full),
            pl.BlockSpec(lw["wqb"].shape, full),
            pl.BlockSpec(lw["wkb"].shape, full),
            pl.BlockSpec(lw["wvb"].shape, full),
            pl.BlockSpec(lw["gains"].shape, full),
            pl.BlockSpec((1, B_Q_RANK), full),
            pl.BlockSpec((1, B_KV_RANK), full),
            pl.BlockSpec((tm, LANES), tab),
            pl.BlockSpec((tm, LANES), tab),
            pl.BlockSpec((tm, LANES), tab),
            pl.BlockSpec((tm, LANES), tab),
        ],
        out_specs=out_specs,
        out_shape=out_shape,
        compiler_params=pltpu.CompilerParams(dimension_semantics=("parallel",), vmem_limit_bytes=VMEM_LIMIT),
        name="in_projection",
    )(x2, lw["g1"], lw["w1"], lw["wqb"], lw["wkb"], lw["wvb"], lw["gains"], lw["gbq"], lw["gbkv"],
      tabs["cosa"], tabs["sina"], tabs["cosb"], tabs["sinb"])
    return outs


def _memkv_kernel(mem_ref, g_ref, w_ref, gain_ref, k_ref, v_ref):
    m = _rms(mem_ref[...], g_ref[...]).astype(MXU_DTYPE)
    kv = _dot(m, w_ref[...])
    half = kv.shape[1] // 2
    k_ref[...] = _cat(_head_norm(kv[:, :half], gain_ref[...], HEAD, HEAD), k_ref.dtype)
    v_ref[...] = _transposed_v(kv[:, half:], v_ref.dtype)


def _memory_kv(mem2, lw):
    n = mem2.shape[0]
    tm = N_MEM
    xq = X_HEADS * HEAD
    row = lambda i: (i, 0)
    full = lambda i: (0, 0)
    return pl.pallas_call(
        _memkv_kernel,
        grid=(n // tm,),
        in_specs=[pl.BlockSpec((tm, D_MODEL), row), pl.BlockSpec((1, D_MODEL), full),
                  pl.BlockSpec((D_MODEL, 2 * xq), full), pl.BlockSpec((1, LANES), full)],
        out_specs=[pl.BlockSpec((tm, xq), row), pl.BlockSpec((X_HEADS * V_ROWS, tm), lambda i: (0, i))],
        out_shape=[jax.ShapeDtypeStruct((n, xq), MXU_DTYPE),
                   jax.ShapeDtypeStruct((X_HEADS * V_ROWS, n), MXU_DTYPE)],
        compiler_params=pltpu.CompilerParams(dimension_semantics=("parallel",)),
        name="memory_kv",
    )(mem2, lw["gmem"], lw["wmem"], lw["gxk"])


def _lane_mask(width, parts, i):
    lane = lax.broadcasted_iota(jnp.int32, (1, width), 1)
    w = width // parts
    return (lane >= i * w) & (lane < (i + 1) * w)


def _flash_kernel(q_ref, k_ref, v_ref, o_ref, qs_ref, s_ref, m_ref, acc_ref, *, mode, nh, tq, tk, n_kv):
    g = pl.program_id(1)
    qt = q_ref[...].astype(jnp.float32).T
    kdim = qs_ref.shape[0]
    for i in range(nh):
        if mode == "gqa":
            blk = qt[i * HEAD:(i + 1) * HEAD, :]
            half = lax.broadcasted_iota(jnp.int32, (kdim, tq), 0) // HEAD
            op = jnp.where(half == g, jnp.concatenate([blk, blk], axis=0), 0.0)
        elif mode == "mha":
            op = qt[i * LANES:(i + 1) * LANES, :]
        else:
            head = lax.broadcasted_iota(jnp.int32, (kdim, tq), 0) // HEAD
            op = jnp.where(head == i, qt, 0.0)
        qs_ref[:, i * tq:(i + 1) * tq] = op.astype(qs_ref.dtype)
    m_ref[...] = jnp.full(m_ref.shape, -jnp.inf, jnp.float32)
    acc_ref[...] = jnp.zeros(acc_ref.shape, jnp.float32)

    def score(j, slot):
        off = pl.multiple_of(j * tk, tk)
        if mode == "mha":
            for i in range(nh):
                cols = slice(i * tq, (i + 1) * tq)
                s_ref[slot, :, cols] = _dot(k_ref[pl.ds(off, tk), i * LANES:(i + 1) * LANES], qs_ref[:, cols])
        else:
            s_ref[slot] = _dot(k_ref[pl.ds(off, tk), :], qs_ref[...])

    def consume(j, slot):
        off = pl.multiple_of(j * tk, tk)
        s = s_ref[slot]
        m_prev = m_ref[0:1, :]
        m_new = jnp.maximum(m_prev, jnp.max(s, axis=0, keepdims=True))
        alpha = jnp.exp2(m_prev - m_new)
        p = jnp.exp2(s - m_new).astype(v_ref.dtype)
        if mode == "gqa":
            pv = _dot(v_ref[:, pl.ds(off, tk)], p)
        else:
            pv = jnp.concatenate(
                [_dot(v_ref[i * V_ROWS:(i + 1) * V_ROWS, pl.ds(off, tk)], p[:, i * tq:(i + 1) * tq])
                 for i in range(nh)], axis=1)
        acc_ref[...] = alpha * acc_ref[...] + pv
        m_ref[0:1, :] = m_new

    score(0, 0)
    if n_kv > 1:
        def pair(jj, carry):
            score(2 * jj + 1, 1)
            consume(2 * jj, 0)
            score(2 * jj + 2, 0)
            consume(2 * jj + 1, 1)
            return carry

        lax.fori_loop(0, n_kv // 2 - 1, pair, 0)
        score(n_kv - 1, 1)
        consume(n_kv - 2, 0)
        consume(n_kv - 1, 1)
    else:
        consume(0, 0)
    a = acc_ref[...]
    ot = a[:HEAD, :] / a[HEAD:HEAD + 1, :]
    o_ref[...] = jnp.concatenate([ot[:, i * tq:(i + 1) * tq] for i in range(nh)], axis=0).T.astype(o_ref.dtype)


def _flash_attention(q, k, vt, *, mode, batch, seq, kv_len, tq, tk):
    kw = 256
    groups = q.shape[1] // kw
    nh = {"gqa": 4, "mha": 2, "cross": 4}[mode]
    kcols = {"gqa": LANES, "mha": kw, "cross": kw}[mode]
    kdim = {"gqa": LANES, "mha": LANES, "cross": kw}[mode]
    vrows = V_ROWS if mode == "gqa" else nh * V_ROWS
    k_map = (lambda b, g, i: (b, 0)) if mode == "gqa" else (lambda b, g, i: (b, g))
    tq = min(tq, seq)
    tk = min(tk, kv_len)
    nq = seq // tq
    ow = nh * HEAD
    kern = functools.partial(_flash_kernel, mode=mode, nh=nh, tq=tq, tk=tk, n_kv=kv_len // tk)
    return pl.pallas_call(
        kern,
        grid=(batch, groups, nq),
        in_specs=[pl.BlockSpec((tq, kw), lambda b, g, i: (b * nq + i, g)),
                  pl.BlockSpec((kv_len, kcols), k_map),
                  pl.BlockSpec((vrows, kv_len), lambda b, g, i: (g, b))],
        out_specs=pl.BlockSpec((tq, ow), lambda b, g, i: (b * nq + i, g)),
        out_shape=jax.ShapeDtypeStruct((batch * seq, groups * ow), MXU_DTYPE),
        scratch_shapes=[pltpu.VMEM((kdim, nh * tq), MXU_DTYPE), pltpu.VMEM((2, tk, nh * tq), jnp.float32),
                        pltpu.VMEM((8, nh * tq), jnp.float32), pltpu.VMEM((V_ROWS, nh * tq), jnp.float32)],
        compiler_params=pltpu.CompilerParams(dimension_semantics=("parallel", "parallel", "parallel"),
                                             vmem_limit_bytes=VMEM_LIMIT),
        name=f"flash_attention_{mode}",
    )(q, k, vt)


def _window_kernel(sink_ref, q_ref, k_ref, v_ref, o_ref, *, tq, seq):
    g = pl.program_id(1)
    j = pl.program_id(2)
    nh = 4
    win = tq + 2 * C_WINDOW
    start = jnp.clip(j * tq - C_WINDOW, 0, seq - win)
    start = pl.multiple_of(start, C_WINDOW)
    k = k_ref[pl.ds(start, win), :]
    v = v_ref[pl.ds(start, win), :]
    q = q_ref[...]
    kw = q.shape[1]
    qs = jnp.concatenate([jnp.where(_lane_mask(kw, nh, i), q, jnp.zeros_like(q)) for i in range(nh)], axis=0)
    s = _dot_nt(qs, k)
    tpos = j * tq + lax.broadcasted_iota(jnp.int32, (tq, win), 0)
    spos = start + lax.broadcasted_iota(jnp.int32, (tq, win), 1)
    rel = jnp.abs(spos - tpos)
    valid = rel <= C_WINDOW
    dist = rel.astype(jnp.float32)
    ps, rs = [], []
    for i in range(nh):
        slope = jnp.where(g == 0, 2.0 ** -(i + 1), 2.0 ** -(i + 5))
        si = jnp.where(valid, s[i * tq:(i + 1) * tq, :] - slope * dist, NEG_INF)
        sink = sink_ref[nh * g + i]
        m = jnp.maximum(jnp.max(si, axis=-1, keepdims=True), sink)
        p = jnp.exp(si - m)
        rs.append(1.0 / (jnp.sum(p, axis=-1, keepdims=True) + jnp.exp(sink - m)))
        ps.append(p.astype(v.dtype))
    pv = _dot(jnp.concatenate(ps, axis=0), v)
    o = jnp.zeros((tq, kw), jnp.float32)
    for i in range(nh):
        o = o + jnp.where(_lane_mask(kw, nh, i), pv[i * tq:(i + 1) * tq, :] * rs[i], 0.0)
    o_ref[...] = o.astype(o_ref.dtype)


def _window_attention(q, k, v, sink, *, batch, seq, tq):
    kw = 256
    groups = q.shape[1] // kw
    nq = seq // tq
    kern = functools.partial(_window_kernel, tq=tq, seq=seq)
    return pl.pallas_call(
        kern,
        grid=(batch, groups, nq),
        in_specs=[pl.BlockSpec(memory_space=pltpu.SMEM),
                  pl.BlockSpec((tq, kw), lambda b, g, i: (b * nq + i, g)),
                  pl.BlockSpec((seq, kw), lambda b, g, i: (b, g)),
                  pl.BlockSpec((seq, kw), lambda b, g, i: (b, g))],
        out_specs=pl.BlockSpec((tq, kw), lambda b, g, i: (b * nq + i, g)),
        out_shape=jax.ShapeDtypeStruct((batch * seq, groups * kw), MXU_DTYPE),
        compiler_params=pltpu.CompilerParams(dimension_semantics=("parallel", "parallel", "parallel"),
                                             vmem_limit_bytes=VMEM_LIMIT),
        name="window_attention",
    )(sink, q, k, v)


def _merge_kernel(x_ref, oa_ref, ob_ref, oc_ref, ox_ref, g1_ref, wg_ref, wbr_ref, wo_ref, g2_ref,
                  wrh_ref, wrl_ref, y_ref, h2_ref, aff_ref):
    x = x_ref[...]
    hb = _rms(x, g1_ref[...]).astype(MXU_DTYPE)
    merged = jnp.zeros(x.shape, jnp.float32)
    r0 = 0
    for i, o_ref in enumerate((oa_ref, ob_ref, oc_ref, ox_ref)):
        r1 = r0 + o_ref.shape[1]
        gate = jax.nn.sigmoid(_dot(hb, wg_ref[:, i * D_MODEL:(i + 1) * D_MODEL]))
        merged = merged + gate * _dot(o_ref[...], wbr_ref[r0:r1, :])
        r0 = r1
    y = x + _dot(merged.astype(MXU_DTYPE), wo_ref[...])
    y_ref[...] = y
    h2 = _rms(y, g2_ref[...])
    h2_hi = h2.astype(MXU_DTYPE)
    h2_lo = (h2 - h2_hi.astype(jnp.float32)).astype(MXU_DTYPE)
    h2_ref[...] = h2_hi
    wrh, wrl = wrh_ref[...], wrl_ref[...]
    logits = _dot_nt(wrh, h2_hi) + _dot_nt(wrh, h2_lo) + _dot_nt(wrl, h2_hi)
    e = jnp.exp(logits - jnp.max(logits, axis=0, keepdims=True))
    aff_ref[...] = e / jnp.sum(e, axis=0, keepdims=True)


def _merge(x2, oa, ob, oc, ox, lw, tm):
    n = x2.shape[0]
    row = lambda i: (i, 0)
    full = lambda i: (0, 0)
    return pl.pallas_call(
        _merge_kernel,
        grid=(n // tm,),
        in_specs=[pl.BlockSpec((tm, D_MODEL), row),
                  pl.BlockSpec((tm, oa.shape[1]), row), pl.BlockSpec((tm, ob.shape[1]), row),
                  pl.BlockSpec((tm, oc.shape[1]), row), pl.BlockSpec((tm, ox.shape[1]), row),
                  pl.BlockSpec((1, D_MODEL), full),
                  pl.BlockSpec(lw["wg"].shape, full), pl.BlockSpec(lw["wbr"].shape, full),
                  pl.BlockSpec(lw["wo"].shape, full), pl.BlockSpec((1, D_MODEL), full),
                  pl.BlockSpec(lw["wrh"].shape, full), pl.BlockSpec(lw["wrl"].shape, full)],
        out_specs=[pl.BlockSpec((tm, D_MODEL), row), pl.BlockSpec((tm, D_MODEL), row),
                   pl.BlockSpec((N_EXPERTS, tm), lambda i: (0, i))],
        out_shape=[jax.ShapeDtypeStruct((n, D_MODEL), jnp.float32),
                   jax.ShapeDtypeStruct((n, D_MODEL), MXU_DTYPE),
                   jax.ShapeDtypeStruct((N_EXPERTS, n), jnp.float32)],
        compiler_params=pltpu.CompilerParams(dimension_semantics=("parallel",), vmem_limit_bytes=VMEM_LIMIT),
        name="merge_router",
    )(x2, oa, ob, oc, ox, lw["g1"], lw["wg"], lw["wbr"], lw["wo"], lw["g2"], lw["wrh"], lw["wrl"])


ROW_TILE = (8, LANES)


def _expert_kernel(xs_ref, wgu_ref, wd_ref, y_ref):
    tc = xs_ref.shape[1]
    gu = _dot(xs_ref[0].reshape(tc, D_MODEL), wgu_ref[0])
    g, u = gu[:, :D_EXPERT], gu[:, D_EXPERT:]
    act = (g * jax.nn.sigmoid(g) * u).astype(MXU_DTYPE)
    y_ref[0] = _dot(act, wd_ref[0]).astype(y_ref.dtype).reshape((tc,) + ROW_TILE)


def _expert_ffn(xs, wgu, wd, cap, tc):
    nt = cap // tc
    return pl.pallas_call(
        _expert_kernel,
        grid=(N_EXPERTS, nt),
        in_specs=[pl.BlockSpec((1, tc) + ROW_TILE, lambda e, c: (e, c, 0, 0)),
                  pl.BlockSpec((1, D_MODEL, 2 * D_EXPERT), lambda e, c: (e, 0, 0)),
                  pl.BlockSpec((1, D_EXPERT, D_MODEL), lambda e, c: (e, 0, 0))],
        out_specs=pl.BlockSpec((1, tc) + ROW_TILE, lambda e, c: (e, c, 0, 0)),
        out_shape=jax.ShapeDtypeStruct((N_EXPERTS, cap) + ROW_TILE, MXU_DTYPE),
        compiler_params=pltpu.CompilerParams(dimension_semantics=("parallel", "parallel"),
                                             vmem_limit_bytes=VMEM_LIMIT),
        name="expert_ffn",
    )(xs, wgu, wd)


CHUNK = 32
MAX_BLOCKS = 128
ROUTE_CHUNK = 256


def _route_kernel(aff_ref, sel_ref, cb_ref, *, cap, n):
    bits = lax.bitcast_convert_type(aff_ref[...], jnp.int32)

    def search(b, thr):
        cand = thr | jnp.left_shift(jnp.int32(1), 30 - b)
        cnt = jnp.sum((bits >= cand).astype(jnp.int32), axis=1, keepdims=True)
        return jnp.where(cnt >= cap, cand, thr)

    thr = lax.fori_loop(0, 31, search, jnp.zeros((N_EXPERTS, 1), jnp.int32))
    n_gt = jnp.sum((bits > thr).astype(jnp.int32), axis=1, keepdims=True)
    need = (cap - n_gt).astype(jnp.float32)
    k = lax.broadcasted_iota(jnp.int32, (ROUTE_CHUNK, ROUTE_CHUNK), 0)
    j = lax.broadcasted_iota(jnp.int32, (ROUTE_CHUNK, ROUTE_CHUNK), 1)
    tri = jnp.where(k <= j, 1.0, 0.0).astype(MXU_DTYPE)

    def chunk(c, carry):
        cg, ce = carry
        off = pl.multiple_of(c * ROUTE_CHUNK, ROUTE_CHUNK)
        b = lax.bitcast_convert_type(aff_ref[:, pl.ds(off, ROUTE_CHUNK)], jnp.int32)
        gt = jnp.where(b > thr, 1.0, 0.0)
        eq = jnp.where(b == thr, 1.0, 0.0)
        pre = _dot(jnp.concatenate([gt, eq], axis=0).astype(MXU_DTYPE), tri)
        pg = pre[:N_EXPERTS] + cg
        pe = pre[N_EXPERTS:] + ce
        before = (pg - gt) + jnp.minimum(pe - eq, need)
        chosen = (gt > 0.0) | ((eq > 0.0) & ((pe - eq) < need))
        before_i = before.astype(jnp.int32)
        cb_ref[:, pl.ds(off, ROUTE_CHUNK)] = before_i
        sel_ref[:, pl.ds(off, ROUTE_CHUNK)] = jnp.where(chosen, before_i, -1)
        return pg[:, ROUTE_CHUNK - 1:ROUTE_CHUNK], pe[:, ROUTE_CHUNK - 1:ROUTE_CHUNK]

    zero = jnp.zeros((N_EXPERTS, 1), jnp.float32)
    lax.fori_loop(0, n // ROUTE_CHUNK, chunk, (zero, zero))


def _route(aff, cap):
    n = aff.shape[1]
    full = lambda i: (0, 0)
    return pl.pallas_call(
        functools.partial(_route_kernel, cap=cap, n=n),
        grid=(1,),
        in_specs=[pl.BlockSpec((N_EXPERTS, n), full)],
        out_specs=[pl.BlockSpec((N_EXPERTS, n), full), pl.BlockSpec((N_EXPERTS, n), full)],
        out_shape=[jax.ShapeDtypeStruct((N_EXPERTS, n), jnp.int32)] * 2,
        compiler_params=pltpu.CompilerParams(dimension_semantics=("arbitrary",), vmem_limit_bytes=VMEM_LIMIT),
        name="route_select",
    )(aff)


def _block_table(ptab_ref, i, tab_e, tab_base, tab_lo, slot, limit):
    def per_expert(e, nb):
        p0 = ptab_ref[e, i]
        nblk = jnp.right_shift(ptab_ref[e, i + 1] - p0 + (CHUNK - 1), 5)

        def per_block(kk, b):
            tab_e[slot, b] = e
            tab_lo[slot, b] = p0 + kk * CHUNK
            tab_base[slot, b] = jnp.minimum(p0 + kk * CHUNK, limit)
            return b + 1

        return lax.fori_loop(0, nblk, per_block, nb)

    return lax.fori_loop(0, N_EXPERTS, per_expert, jnp.int32(0))


def _dispatch_kernel(ptab_ref, sel_ref, h_ref, xs_ref, oh_ref, res_ref, tab_e, tab_base, tab_lo, cnt_ref, sem, *,
                     cap, tm):
    i = pl.program_id(0)
    last = pl.num_programs(0) - 1

    def chunk_copy(b):
        e = tab_e[0, b]
        base = tab_base[0, b]
        per = tm // CHUNK
        src = res_ref.at[b // per, pl.ds(pl.multiple_of(lax.rem(b, per) * CHUNK, CHUNK), CHUNK)]
        return pltpu.make_async_copy(src, xs_ref.at[e, pl.ds(base, CHUNK)], sem.at[0])

    def wait_all(nb):
        def w(b, c):
            chunk_copy(b).wait()
            return c
        lax.fori_loop(0, nb, w, 0)

    @pl.when(i == 0)
    def _():
        oh_ref[...] = jnp.zeros(oh_ref.shape, oh_ref.dtype)
        cnt_ref[0] = 0
        res_ref[0] = jnp.zeros(res_ref.shape[1:], res_ref.dtype)
        pads = [pltpu.make_async_copy(res_ref.at[0, pl.ds(0, CHUNK)], xs_ref.at[e, pl.ds(cap, CHUNK)], sem.at[0])
                for e in range(N_EXPERTS)]
        for c in pads:
            c.start()
        for c in pads:
            c.wait()

    wait_all(cnt_ref[0])
    nb = _block_table(ptab_ref, i, tab_e, tab_base, tab_lo, 0, cap)

    def build(b, c):
        e = tab_e[0, b]
        slots = tab_base[0, b] + lax.broadcasted_iota(jnp.int32, (CHUNK, tm), 0)
        hit = sel_ref[pl.ds(e, 1), :] == slots
        oh_ref[pl.ds(pl.multiple_of(b * CHUNK, CHUNK), CHUNK), :] = jnp.where(hit, 1.0, 0.0).astype(oh_ref.dtype)
        return c

    lax.fori_loop(0, nb, build, 0)

    def group(gi, c):
        rows = pl.ds(pl.multiple_of(gi * tm, tm), tm)
        res_ref[gi] = _dot(oh_ref[rows, :], h_ref[...]).astype(res_ref.dtype).reshape((tm,) + ROW_TILE)
        return c

    lax.fori_loop(0, jnp.right_shift(nb * CHUNK + tm - 1, tm.bit_length() - 1), group, 0)

    def issue(b, c):
        chunk_copy(b).start()
        return c

    lax.fori_loop(0, nb, issue, 0)
    cnt_ref[0] = nb

    @pl.when(i == last)
    def _():
        wait_all(nb)
        cnt_ref[0] = 0


def _dispatch(ptab, sel, h2, cap, tm):
    n = h2.shape[0]
    grid_spec = pltpu.PrefetchScalarGridSpec(
        num_scalar_prefetch=1,
        grid=(n // tm,),
        in_specs=[pl.BlockSpec((N_EXPERTS, tm), lambda i, p: (0, i)),
                  pl.BlockSpec((tm, D_MODEL), lambda i, p: (i, 0))],
        out_specs=pl.BlockSpec(memory_space=pl.ANY),
        scratch_shapes=[pltpu.VMEM((MAX_BLOCKS * CHUNK, tm), MXU_DTYPE),
                        pltpu.VMEM((MAX_BLOCKS * CHUNK // tm, tm) + ROW_TILE, MXU_DTYPE),
                        pltpu.SMEM((1, MAX_BLOCKS), jnp.int32), pltpu.SMEM((1, MAX_BLOCKS), jnp.int32),
                        pltpu.SMEM((1, MAX_BLOCKS), jnp.int32),
                        pltpu.SMEM((1,), jnp.int32), pltpu.SemaphoreType.DMA((1,))],
    )
    return pl.pallas_call(
        functools.partial(_dispatch_kernel, cap=cap, tm=tm),
        grid_spec=grid_spec,
        out_shape=jax.ShapeDtypeStruct((N_EXPERTS, cap + CHUNK) + ROW_TILE, MXU_DTYPE),
        compiler_params=pltpu.CompilerParams(dimension_semantics=("arbitrary",), vmem_limit_bytes=VMEM_LIMIT),
        name="moe_dispatch",
    )(ptab, sel, h2)


def _combine_kernel(ptab_ref, sel_ref, aff_ref, y_ref, ye_ref, o_ref, stage_ref, wt_ref, tab_e, tab_base, tab_lo,
                    cnt_ref, sem, *, cap, tm):
    i = pl.program_id(0)
    nt = pl.num_programs(0)
    slot = lax.rem(i, 2)

    def chunk_copy(s, b):
        e = tab_e[s, b]
        base = tab_base[s, b]
        dst = stage_ref.at[s, pl.ds(pl.multiple_of(b * CHUNK, CHUNK), CHUNK)]
        return pltpu.make_async_copy(ye_ref.at[e, pl.ds(base, CHUNK)], dst, sem.at[s])

    def fetch(tile, s):
        nb = _block_table(ptab_ref, tile, tab_e, tab_base, tab_lo, s, cap - CHUNK)
        cnt_ref[s] = nb

        def issue(b, c):
            chunk_copy(s, b).start()
            return c

        lax.fori_loop(0, nb, issue, 0)

    @pl.when(i == 0)
    def _():
        stage_ref[...] = jnp.zeros(stage_ref.shape, stage_ref.dtype)
        fetch(i, slot)

    @pl.when(i + 1 < nt)
    def _():
        fetch(i + 1, 1 - slot)

    nb = cnt_ref[slot]

    def wait(b, c):
        chunk_copy(slot, b).wait()
        return c

    lax.fori_loop(0, nb, wait, 0)
    ngroups = jnp.right_shift(nb * CHUNK + tm - 1, tm.bit_length() - 1)

    def build(b, c):
        e = tab_e[slot, b]
        slots = tab_base[slot, b] + lax.broadcasted_iota(jnp.int32, (CHUNK, tm), 0)
        hit = (sel_ref[pl.ds(e, 1), :] == slots) & (slots >= tab_lo[slot, b])
        w = jnp.where(hit, aff_ref[pl.ds(e, 1), :], 0.0)
        wt_ref[pl.ds(pl.multiple_of(b * CHUNK, CHUNK), CHUNK), :] = w.astype(wt_ref.dtype)
        return c

    def clear(b, c):
        wt_ref[pl.ds(pl.multiple_of(b * CHUNK, CHUNK), CHUNK), :] = jnp.zeros((CHUNK, tm), wt_ref.dtype)
        return c

    lax.fori_loop(0, nb, build, 0)
    lax.fori_loop(nb, ngroups * (tm // CHUNK), clear, 0)
    o_ref[...] = y_ref[...]

    def group(gi, c):
        rows = pl.ds(pl.multiple_of(gi * tm, tm), tm)
        o_ref[...] += lax.dot_general(wt_ref[rows, :], stage_ref[slot, rows].reshape(tm, D_MODEL),
                                      (((0,), (0,)), ((), ())), preferred_element_type=jnp.float32)
        return c

    lax.fori_loop(0, ngroups, group, 0)


def _combine(ptab, sel, aff, y, ye, cap, tm):
    n = y.shape[0]
    grid_spec = pltpu.PrefetchScalarGridSpec(
        num_scalar_prefetch=1,
        grid=(n // tm,),
        in_specs=[pl.BlockSpec((N_EXPERTS, tm), lambda i, p: (0, i)),
                  pl.BlockSpec((N_EXPERTS, tm), lambda i, p: (0, i)),
                  pl.BlockSpec((tm, D_MODEL), lambda i, p: (i, 0)),
                  pl.BlockSpec(memory_space=pl.ANY)],
        out_specs=pl.BlockSpec((tm, D_MODEL), lambda i, p: (i, 0)),
        scratch_shapes=[pltpu.VMEM((2, MAX_BLOCKS * CHUNK) + ROW_TILE, MXU_DTYPE),
                        pltpu.VMEM((MAX_BLOCKS * CHUNK, tm), MXU_DTYPE),
                        pltpu.SMEM((2, MAX_BLOCKS), jnp.int32), pltpu.SMEM((2, MAX_BLOCKS), jnp.int32),
                        pltpu.SMEM((2, MAX_BLOCKS), jnp.int32),
                        pltpu.SMEM((2,), jnp.int32), pltpu.SemaphoreType.DMA((2,))],
    )
    return pl.pallas_call(
        functools.partial(_combine_kernel, cap=cap, tm=tm),
        grid_spec=grid_spec,
        out_shape=jax.ShapeDtypeStruct((n, D_MODEL), jnp.float32),
        compiler_params=pltpu.CompilerParams(dimension_semantics=("arbitrary",), vmem_limit_bytes=VMEM_LIMIT),
        name="moe_combine",
    )(ptab, sel, aff, y, ye)


def _rope_tables(seq):
    pos = jnp.arange(seq, dtype=jnp.int32)
    inv = jnp.power(ROPE_THETA, -jnp.arange(0, 32, 2, dtype=jnp.float32) / 32.0)
    ang_row = (pos // GRID_W).astype(jnp.float32)[:, None] * inv[None, :]
    ang_col = (pos % GRID_W).astype(jnp.float32)[:, None] * inv[None, :]
    ang_seq = pos.astype(jnp.float32)[:, None] * inv[None, :]
    sign = jnp.concatenate([-jnp.ones((16,), jnp.float32), jnp.ones((16,), jnp.float32)])

    def pair(ang):
        return jnp.tile(jnp.cos(ang), (1, 2)), jnp.tile(jnp.sin(ang), (1, 2)) * sign[None, :]

    cr, sr = pair(ang_row)
    cc, sc = pair(ang_col)
    cs, ss = pair(ang_seq)
    one = jnp.ones((seq, 32), jnp.float32)
    zero = jnp.zeros((seq, 32), jnp.float32)
    return {
        "cosa": jnp.concatenate([cr, cc, cr, cc], axis=1),
        "sina": jnp.concatenate([sr, sc, sr, sc], axis=1),
        "cosb": jnp.concatenate([one, one, cs, one], axis=1),
        "sinb": jnp.concatenate([zero, zero, ss, zero], axis=1),
    }


def _prep_layer(l, p):
    f32 = jnp.float32
    w_in = p["w_in"][l]
    a0, b0, c0, x0, g0 = 0, 768, 1440, 2208, 2464

    def dup2(w):
        return jnp.concatenate([w[:, :64], w[:, :64], w[:, 64:], w[:, 64:]], axis=1)

    kr = jnp.zeros((D_MODEL, LANES), f32).at[:, 64:96].set(w_in[:, b0 + 640:b0 + 672])
    w1 = jnp.concatenate([
        w_in[:, a0:a0 + 768],
        w_in[:, b0:b0 + 384], w_in[:, b0 + 384:b0 + 640],
        w_in[:, c0:c0 + 512], dup2(w_in[:, c0 + 512:c0 + 640]), dup2(w_in[:, c0 + 640:c0 + 768]),
        w_in[:, x0:x0 + 256], kr], axis=1)
    assert w1.shape[1] == _C_END

    wqb = jnp.zeros((B_Q_RANK, B_HEADS, LANES), f32).at[:, :, :B_QK].set(
        p["b_w_q_b"][l].reshape(B_Q_RANK, B_HEADS, B_QK)).reshape(B_Q_RANK, B_HEADS * LANES)
    wkv = p["b_w_kv_b"][l].reshape(B_KV_RANK, B_HEADS, 2 * B_NOPE)
    wkb = jnp.zeros((B_KV_RANK, B_HEADS, LANES), f32).at[:, :, :B_NOPE].set(
        wkv[:, :, :B_NOPE]).reshape(B_KV_RANK, B_HEADS * LANES)
    wvb = wkv[:, :, B_NOPE:].reshape(B_KV_RANK, B_HEADS * B_NOPE)

    def g64(g, scale=1.0):
        return jnp.tile(g * scale, 2)

    def g96(g, scale=1.0):
        return jnp.concatenate([g * scale, jnp.zeros((LANES - B_QK,), f32)])

    gains = jnp.stack([
        g64(p["a_q_norm"][l], HEAD ** -0.5 * LOG2E), g64(p["a_k_norm"][l]),
        g96(p["b_q_norm"][l], B_QK ** -0.5 * LOG2E), g96(p["b_k_norm"][l]),
        g64(p["c_q_norm"][l], HEAD ** -0.5), g64(p["c_k_norm"][l]),
        g64(p["x_q_norm"][l], HEAD ** -0.5 * LOG2E), jnp.zeros((LANES,), f32)])

    wr = p["w_router"][l].T
    wrh = wr.astype(MXU_DTYPE)
    wrl = (wr - wrh.astype(f32)).astype(MXU_DTYPE)
    bf = lambda a: a.astype(MXU_DTYPE)
    return {
        "g1": p["norm1_g"][l][None, :], "w1": bf(w1), "wqb": bf(wqb), "wkb": bf(wkb), "wvb": bf(wvb),
        "gains": gains, "gbq": p["b_q_a_norm"][l][None, :], "gbkv": p["b_kv_a_norm"][l][None, :],
        "gmem": p["mem_norm_g"][l][None, :], "wmem": bf(p["w_mem_kv"][l]), "gxk": g64(p["x_k_norm"][l])[None, :],
        "sink": p["c_sink"][l],
        "wg": bf(w_in[:, g0:]), "wbr": bf(p["w_branch"][l]), "wo": bf(p["w_out"][l]),
        "g2": p["norm2_g"][l][None, :], "wrh": wrh, "wrl": wrl,
        "wgu": bf(p["w_gate_up"][l]), "wd": bf(p["w_down"][l]),
    }


def _trunk(x, mem, layers):
    batch, seq, _ = x.shape
    n = batch * seq
    tm = min(256, seq)
    cap = EC_FACTOR * n // N_EXPERTS
    tc = min(256, cap)
    tabs = _rope_tables(seq)
    x2 = x.reshape(n, D_MODEL)
    mem2 = mem.reshape(batch * N_MEM, D_MODEL)
    for lw in layers:
        qa, ka, va, qb, kb, vb, qc, kc, vc, qx = _in_projection(x2, lw, tabs, seq, tm)
        kx, vx = _memory_kv(mem2, lw)
        oa = _flash_attention(qa, ka, va, mode="gqa", batch=batch, seq=seq, kv_len=seq, tq=TQ_A, tk=TK_A)
        ob = _flash_attention(qb, kb, vb, mode="mha", batch=batch, seq=seq, kv_len=seq, tq=TQ_B, tk=TK_B)
        oc = _window_attention(qc, kc, vc, lw["sink"], batch=batch, seq=seq, tq=min(256, seq - 2 * C_WINDOW))
        ox = _flash_attention(qx, kx, vx, mode="cross", batch=batch, seq=seq, kv_len=N_MEM, tq=TQ_X, tk=N_MEM)
        y, h2, aff = _merge(x2, oa, ob, oc, ox, lw, tm)
        sel, before = _route(aff, cap)
        ptab = jnp.concatenate([before[:, ::tm], jnp.full((N_EXPERTS, 1), cap, jnp.int32)], axis=1)
        xs = _dispatch(ptab, sel, h2, cap, tm)
        ye = _expert_ffn(xs, lw["wgu"], lw["wd"], cap, tc)
        x2 = _combine(ptab, sel, aff, y, ye, cap, tm)
    return x2.reshape(batch, seq, D_MODEL)


def kernel(x_prompt, x_sample, mem_prompt, mem_sample, norm1_g, w_in, a_q_norm, a_k_norm, b_q_a_norm,
           b_w_q_b, b_kv_a_norm, b_w_kv_b, b_q_norm, b_k_norm, c_q_norm, c_k_norm, c_sink, x_q_norm,
           x_k_norm, mem_norm_g, w_mem_kv, w_branch, w_out, norm2_g, w_router, w_gate_up, w_down):
    p = dict(norm1_g=norm1_g, w_in=w_in, a_q_norm=a_q_norm, a_k_norm=a_k_norm, b_q_a_norm=b_q_a_norm,
             b_w_q_b=b_w_q_b, b_kv_a_norm=b_kv_a_norm, b_w_kv_b=b_w_kv_b, b_q_norm=b_q_norm, b_k_norm=b_k_norm,
             c_q_norm=c_q_norm, c_k_norm=c_k_norm, c_sink=c_sink, x_q_norm=x_q_norm, x_k_norm=x_k_norm,
             mem_norm_g=mem_norm_g, w_mem_kv=w_mem_kv, w_branch=w_branch, w_out=w_out, norm2_g=norm2_g,
             w_router=w_router, w_gate_up=w_gate_up, w_down=w_down)
    layers = [_prep_layer(l, p) for l in range(w_in.shape[0])]
    return (_trunk(x_prompt, mem_prompt, layers), _trunk(x_sample, mem_sample, layers))
```

```python
import functools

import jax
import jax.numpy as jnp
from jax import lax
from jax.experimental import pallas as pl
from jax.experimental.pallas import tpu as pltpu

D_MODEL = 1024
GRID_W = 64
N_MEM = 256
EPS = 1e-6
ROPE_THETA = 10000.0
NEG_INF = -1e30
HEAD = 64
A_KV_HEADS = 2
B_HEADS = 8
B_Q_RANK = 384
B_KV_RANK = 256
B_NOPE = 64
B_ROPE = 32
B_QK = B_NOPE + B_ROPE
C_WINDOW = 128
X_HEADS = 4
N_EXPERTS = 16
EC_FACTOR = 2
D_EXPERT = 1024
LANES = 128

MXU_DTYPE = jnp.bfloat16
VMEM_LIMIT = 56 * 1024 * 1024
LOG2E = 1.4426950408889634
TQ_A, TK_A = 512, 512
TQ_B, TK_B = 1024, 512
TQ_X = 256

_C_AQ, _C_AK, _C_AV = 0, 512, 640
_C_BCQ, _C_BCKV = 768, 1152
_C_CQ, _C_CK, _C_CV = 1408, 1920, 2048
_C_XQ, _C_KR, _C_END = 2176, 2432, 2560
V_ROWS = 80


def _dot(a, b):
    return jnp.dot(a, b, preferred_element_type=jnp.float32)


def _dot_nt(a, b):
    return lax.dot_general(a, b, (((1,), (1,)), ((), ())), preferred_element_type=jnp.float32)


def _rms(x, g):
    return x * lax.rsqrt(jnp.mean(x * x, axis=-1, keepdims=True) + EPS) * g


def _block_ones(head_w):
    w = 2 * LANES
    row = lax.broadcasted_iota(jnp.int32, (w, w), 0) // head_w
    col = lax.broadcasted_iota(jnp.int32, (w, w), 1) // head_w
    return jnp.where(row == col, 1.0, 0.0).astype(MXU_DTYPE)


def _head_norm(z, gain, ones, n_valid):
    width = z.shape[1]
    outs = []
    c = 0
    while c < width:
        w = min(2 * LANES, width - c)
        blk = z[:, c:c + w]
        ss = _dot((blk * blk).astype(MXU_DTYPE), ones[:w, :w])
        g = gain if w == LANES else jnp.concatenate([gain, gain], axis=1)
        normed = blk * lax.rsqrt(ss * (1.0 / n_valid) + EPS) * g
        outs += [normed[:, i * LANES:(i + 1) * LANES] for i in range(w // LANES)]
        c += w
    return outs


def _rope(blocks, cos, sin_signed):
    lane = lax.broadcasted_iota(jnp.int32, (1, LANES), 1)
    first = (lane % 32) < 16
    outs = []
    for blk in blocks:
        up = pltpu.roll(blk, LANES - 16, 1)
        dn = pltpu.roll(blk, 16, 1)
        outs.append(blk * cos + jnp.where(first, up, dn) * sin_signed)
    return outs


def _cat(blocks, dtype):
    return jnp.concatenate(blocks, axis=1).astype(dtype)


def _transposed_v(v, dtype):
    vt = v.T
    ones = jnp.ones((V_ROWS - HEAD, v.shape[0]), jnp.float32)
    parts = []
    for h in range(v.shape[1] // HEAD):
        parts += [vt[h * HEAD:(h + 1) * HEAD, :], ones]
    return jnp.concatenate(parts, axis=0).astype(dtype)


def _inproj_kernel(x_ref, g1_ref, w1_ref, wqb_ref, wkb_ref, wvb_ref, gains_ref, gbq_ref, gbkv_ref,
                   cosa_ref, sina_ref, cosb_ref, sinb_ref,
                   qa_ref, ka_ref, va_ref, qb_ref, kb_ref, vb_ref, qc_ref, kc_ref, vc_ref, qx_ref):
    x = x_ref[...]
    hb = _rms(x, g1_ref[...]).astype(MXU_DTYPE)
    z = _dot(hb, w1_ref[...])
    gains = gains_ref[...]
    cosa, sina = cosa_ref[...], sina_ref[...]
    cosb, sinb = cosb_ref[...], sinb_ref[...]
    od = qa_ref.dtype
    ones64, ones128 = _block_ones(HEAD), _block_ones(LANES)

    qa = _rope(_head_norm(z[:, _C_AQ:_C_AK], gains[0:1], ones64, HEAD), cosa, sina)
    qa_ref[...] = _cat(qa, od)
    ka = _rope(_head_norm(z[:, _C_AK:_C_AV], gains[1:2], ones64, HEAD), cosa, sina)
    ka_ref[...] = ka[0].astype(od)
    va_ref[...] = _transposed_v(z[:, _C_AV:_C_BCQ], od)

    cq = _rms(z[:, _C_BCQ:_C_BCKV], gbq_ref[...]).astype(MXU_DTYPE)
    ckv = _rms(z[:, _C_BCKV:_C_CQ], gbkv_ref[...]).astype(MXU_DTYPE)
    q8 = _dot(cq, wqb_ref[...])
    k8 = _dot(ckv, wkb_ref[...])
    kr = z[:, _C_KR:_C_END]
    k8 = k8 + jnp.concatenate([kr] * B_HEADS, axis=1)
    qb_ref[...] = _cat(_rope(_head_norm(q8, gains[2:3], ones128, B_QK), cosb, sinb), od)
    kb_ref[...] = _cat(_rope(_head_norm(k8, gains[3:4], ones128, B_QK), cosb, sinb), od)
    vb_ref[...] = _transposed_v(_dot(ckv, wvb_ref[...]), od)

    qc_ref[...] = _cat(_head_norm(z[:, _C_CQ:_C_CK], gains[4:5], ones64, HEAD), od)
    kc_ref[...] = _head_norm(z[:, _C_CK:_C_CV], gains[5:6], ones64, HEAD)[0].astype(od)
    vc_ref[...] = _transposed_v(z[:, _C_CV:_C_XQ], od)

    qx_ref[...] = _cat(_head_norm(z[:, _C_XQ:_C_KR], gains[6:7], ones64, HEAD), od)


def _in_projection(x2, lw, tabs, seq, tm):
    n = x2.shape[0]
    nt = seq // tm
    row = lambda i: (i, 0)
    full = lambda i: (0, 0)
    tab = lambda i: (i % nt, 0)
    col = lambda i: (0, i)
    outs_def = [(512, False), (LANES, False), (A_KV_HEADS * V_ROWS, True), (1024, False), (1024, False),
                (B_HEADS * V_ROWS, True), (512, False), (LANES, False), (A_KV_HEADS * V_ROWS, True), (256, False)]
    out_specs = [pl.BlockSpec((w, tm), col) if t else pl.BlockSpec((tm, w), row) for w, t in outs_def]
    out_shape = [jax.ShapeDtypeStruct((w, n) if t else (n, w), MXU_DTYPE) for w, t in outs_def]
    outs = pl.pallas_call(
        _inproj_kernel,
        grid=(n // tm,),
        in_specs=[
            pl.BlockSpec((tm, D_MODEL), row),
            pl.BlockSpec((1, D_MODEL), full),
            pl.BlockSpec(lw["w1"].shape, full),
            pl.BlockSpec(lw["wqb"].shape, full),
            pl.BlockSpec(lw["wkb"].shape, full),
            pl.BlockSpec(lw["wvb"].shape, full),
            pl.BlockSpec(lw["gains"].shape, full),
            pl.BlockSpec((1, B_Q_RANK), full),
            pl.BlockSpec((1, B_KV_RANK), full),
            pl.BlockSpec((tm, LANES), tab),
            pl.BlockSpec((tm, LANES), tab),
            pl.BlockSpec((tm, LANES), tab),
            pl.BlockSpec((tm, LANES), tab),
        ],
        out_specs=out_specs,
        out_shape=out_shape,
        compiler_params=pltpu.CompilerParams(dimension_semantics=("parallel",), vmem_limit_bytes=VMEM_LIMIT),
        name="in_projection",
    )(x2, lw["g1"], lw["w1"], lw["wqb"], lw["wkb"], lw["wvb"], lw["gains"], lw["gbq"], lw["gbkv"],
      tabs["cosa"], tabs["sina"], tabs["cosb"], tabs["sinb"])
    return outs


def _memkv_kernel(mem_ref, g_ref, w_ref, gain_ref, k_ref, v_ref):
    m = _rms(mem_ref[...], g_ref[...]).astype(MXU_DTYPE)
    kv = _dot(m, w_ref[...])
    half = kv.shape[1] // 2
    k_ref[...] = _cat(_head_norm(kv[:, :half], gain_ref[...], _block_ones(HEAD), HEAD), k_ref.dtype)
    v_ref[...] = _transposed_v(kv[:, half:], v_ref.dtype)


def _memory_kv(mem2, lw):
    n = mem2.shape[0]
    tm = N_MEM
    xq = X_HEADS * HEAD
    row = lambda i: (i, 0)
    full = lambda i: (0, 0)
    return pl.pallas_call(
        _memkv_kernel,
        grid=(n // tm,),
        in_specs=[pl.BlockSpec((tm, D_MODEL), row), pl.BlockSpec((1, D_MODEL), full),
                  pl.BlockSpec((D_MODEL, 2 * xq), full), pl.BlockSpec((1, LANES), full)],
        out_specs=[pl.BlockSpec((tm, xq), row), pl.BlockSpec((X_HEADS * V_ROWS, tm), lambda i: (0, i))],
        out_shape=[jax.ShapeDtypeStruct((n, xq), MXU_DTYPE),
                   jax.ShapeDtypeStruct((X_HEADS * V_ROWS, n), MXU_DTYPE)],
        compiler_params=pltpu.CompilerParams(dimension_semantics=("parallel",)),
        name="memory_kv",
    )(mem2, lw["gmem"], lw["wmem"], lw["gxk"])


def _flash_kernel(q_ref, k_ref, v_ref, o_ref, qs_ref, s_ref, m_ref, acc_ref, *, mode, nh, tq, tk, n_kv):
    g = pl.program_id(1)
    qt = q_ref[...].astype(jnp.float32).T
    kdim = qs_ref.shape[0]
    for i in range(nh):
        if mode == "gqa":
            blk = qt[i * HEAD:(i + 1) * HEAD, :]
            half = lax.broadcasted_iota(jnp.int32, (kdim, tq), 0) // HEAD
            op = jnp.where(half == g, jnp.concatenate([blk, blk], axis=0), 0.0)
        elif mode == "mha":
            op = qt[i * LANES:(i + 1) * LANES, :]
        else:
            head = lax.broadcasted_iota(jnp.int32, (kdim, tq), 0) // HEAD
            op = jnp.where(head == i, qt, 0.0)
        qs_ref[:, i * tq:(i + 1) * tq] = op.astype(qs_ref.dtype)
    m_ref[...] = jnp.full(m_ref.shape, -jnp.inf, jnp.float32)
    acc_ref[...] = jnp.zeros(acc_ref.shape, jnp.float32)

    def score(j, slot):
        off = pl.multiple_of(j * tk, tk)
        if mode == "mha":
            for i in range(nh):
                cols = slice(i * tq, (i + 1) * tq)
                s_ref[slot, :, cols] = _dot(k_ref[pl.ds(off, tk), i * LANES:(i + 1) * LANES], qs_ref[:, cols])
        else:
            s_ref[slot] = _dot(k_ref[pl.ds(off, tk), :], qs_ref[...])

    def consume(j, slot):
        off = pl.multiple_of(j * tk, tk)
        s = s_ref[slot]
        m_prev = m_ref[0:1, :]
        m_new = jnp.maximum(m_prev, jnp.max(s, axis=0, keepdims=True))
        alpha = jnp.exp2(m_prev - m_new)
        p = jnp.exp2(s - m_new).astype(v_ref.dtype)
        if mode == "gqa":
            pv = _dot(v_ref[:, pl.ds(off, tk)], p)
        else:
            pv = jnp.concatenate(
                [_dot(v_ref[i * V_ROWS:(i + 1) * V_ROWS, pl.ds(off, tk)], p[:, i * tq:(i + 1) * tq])
                 for i in range(nh)], axis=1)
        acc_ref[...] = alpha * acc_ref[...] + pv
        m_ref[0:1, :] = m_new

    score(0, 0)
    if n_kv > 1:
        def pair(jj, carry):
            score(2 * jj + 1, 1)
            consume(2 * jj, 0)
            score(2 * jj + 2, 0)
            consume(2 * jj + 1, 1)
            return carry

        lax.fori_loop(0, n_kv // 2 - 1, pair, 0)
        score(n_kv - 1, 1)
        consume(n_kv - 2, 0)
        consume(n_kv - 1, 1)
    else:
        consume(0, 0)
    a = acc_ref[...]
    ot = a[:HEAD, :] / a[HEAD:HEAD + 1, :]
    o_ref[...] = jnp.concatenate([ot[:, i * tq:(i + 1) * tq] for i in range(nh)], axis=0).T.astype(o_ref.dtype)


def _flash_attention(q, k, vt, *, mode, batch, seq, kv_len, tq, tk):
    kw = 256
    groups = q.shape[1] // kw
    nh = {"gqa": 4, "mha": 2, "cross": 4}[mode]
    kcols = {"gqa": LANES, "mha": kw, "cross": kw}[mode]
    kdim = {"gqa": LANES, "mha": LANES, "cross": kw}[mode]
    vrows = V_ROWS if mode == "gqa" else nh * V_ROWS
    k_map = (lambda b, g, i: (b, 0)) if mode == "gqa" else (lambda b, g, i: (b, g))
    tq = min(tq, seq)
    tk = min(tk, kv_len)
    nq = seq // tq
    ow = nh * HEAD
    kern = functools.partial(_flash_kernel, mode=mode, nh=nh, tq=tq, tk=tk, n_kv=kv_len // tk)
    return pl.pallas_call(
        kern,
        grid=(batch, groups, nq),
        in_specs=[pl.BlockSpec((tq, kw), lambda b, g, i: (b * nq + i, g)),
                  pl.BlockSpec((kv_len, kcols), k_map),
                  pl.BlockSpec((vrows, kv_len), lambda b, g, i: (g, b))],
        out_specs=pl.BlockSpec((tq, ow), lambda b, g, i: (b * nq + i, g)),
        out_shape=jax.ShapeDtypeStruct((batch * seq, groups * ow), MXU_DTYPE),
        scratch_shapes=[pltpu.VMEM((kdim, nh * tq), MXU_DTYPE), pltpu.VMEM((2, tk, nh * tq), jnp.float32),
                        pltpu.VMEM((8, nh * tq), jnp.float32), pltpu.VMEM((V_ROWS, nh * tq), jnp.float32)],
        compiler_params=pltpu.CompilerParams(dimension_semantics=("parallel", "parallel", "parallel"),
                                             vmem_limit_bytes=VMEM_LIMIT),
        name=f"flash_attention_{mode}",
    )(q, k, vt)


def _window_kernel(sink_ref, q_ref, k_ref, v_ref, o_ref, *, tq, seq):
    g = pl.program_id(1)
    j = pl.program_id(2)
    nh = 4
    win = tq + 2 * C_WINDOW
    start = pl.multiple_of(jnp.clip(j * tq - C_WINDOW, 0, seq - win), C_WINDOW)
    k = k_ref[pl.ds(start, win), :]
    vt = v_ref[:, pl.ds(start, win)]
    qt = q_ref[...].astype(jnp.float32).T
    half = lax.broadcasted_iota(jnp.int32, (2 * HEAD, tq), 0) // HEAD
    ops = []
    for i in range(nh):
        blk = qt[i * HEAD:(i + 1) * HEAD, :]
        ops.append(jnp.where(half == g, jnp.concatenate([blk, blk], axis=0), 0.0))
    s = _dot(k, jnp.concatenate(ops, axis=1).astype(k.dtype))
    spos = start + lax.broadcasted_iota(jnp.int32, (win, tq), 0)
    tpos = j * tq + lax.broadcasted_iota(jnp.int32, (win, tq), 1)
    rel = jnp.abs(spos - tpos)
    dist = jnp.where(rel <= C_WINDOW, rel.astype(jnp.float32), -NEG_INF)
    ps, sinks = [], []
    for i in range(nh):
        slope = jnp.where(g == 0, 2.0 ** -(i + 1), 2.0 ** -(i + 5)) * LOG2E
        si = s[:, i * tq:(i + 1) * tq] - slope * dist
        sink = sink_ref[nh * g + i] * LOG2E
        m = jnp.maximum(jnp.max(si, axis=0, keepdims=True), sink)
        ps.append(jnp.exp2(si - m).astype(vt.dtype))
        sinks.append(jnp.exp2(sink - m))
    acc = _dot(vt, jnp.concatenate(ps, axis=1))
    ot = acc[:HEAD, :] / (acc[HEAD:HEAD + 1, :] + jnp.concatenate(sinks, axis=1))
    o_ref[...] = jnp.concatenate([ot[:, i * tq:(i + 1) * tq] for i in range(nh)], axis=0).T.astype(o_ref.dtype)


def _window_attention(q, k, v, sink, *, batch, seq, tq):
    kw = 256
    groups = q.shape[1] // kw
    nq = seq // tq
    kern = functools.partial(_window_kernel, tq=tq, seq=seq)
    return pl.pallas_call(
        kern,
        grid=(batch, groups, nq),
        in_specs=[pl.BlockSpec(memory_space=pltpu.SMEM),
                  pl.BlockSpec((tq, kw), lambda b, g, i: (b * nq + i, g)),
                  pl.BlockSpec((seq, LANES), lambda b, g, i: (b, 0)),
                  pl.BlockSpec((V_ROWS, seq), lambda b, g, i: (g, b))],
        out_specs=pl.BlockSpec((tq, kw), lambda b, g, i: (b * nq + i, g)),
        out_shape=jax.ShapeDtypeStruct((batch * seq, groups * kw), MXU_DTYPE),
        compiler_params=pltpu.CompilerParams(dimension_semantics=("parallel", "parallel", "parallel"),
                                             vmem_limit_bytes=VMEM_LIMIT),
        name="window_attention",
    )(sink, q, k, v)


def _merge_kernel(x_ref, oa_ref, ob_ref, oc_ref, ox_ref, g1_ref, wg_ref, wbr_ref, wo_ref, g2_ref,
                  wrh_ref, wrl_ref, y_ref, h2_ref, aff_ref):
    x = x_ref[...]
    hb = _rms(x, g1_ref[...]).astype(MXU_DTYPE)
    merged = jnp.zeros(x.shape, jnp.float32)
    r0 = 0
    for i, o_ref in enumerate((oa_ref, ob_ref, oc_ref, ox_ref)):
        r1 = r0 + o_ref.shape[1]
        gate = jax.nn.sigmoid(_dot(hb, wg_ref[:, i * D_MODEL:(i + 1) * D_MODEL]))
        merged = merged + gate * _dot(o_ref[...], wbr_ref[r0:r1, :])
        r0 = r1
    y = x + _dot(merged.astype(MXU_DTYPE), wo_ref[...])
    y_ref[...] = y
    h2 = _rms(y, g2_ref[...])
    h2_hi = h2.astype(MXU_DTYPE)
    h2_lo = (h2 - h2_hi.astype(jnp.float32)).astype(MXU_DTYPE)
    h2_ref[...] = h2_hi
    wrh, wrl = wrh_ref[...], wrl_ref[...]
    logits = _dot_nt(wrh, h2_hi) + _dot_nt(wrh, h2_lo) + _dot_nt(wrl, h2_hi)
    e = jnp.exp(logits - jnp.max(logits, axis=0, keepdims=True))
    aff_ref[...] = e / jnp.sum(e, axis=0, keepdims=True)


def _merge(x2, oa, ob, oc, ox, lw, tm):
    n = x2.shape[0]
    row = lambda i: (i, 0)
    full = lambda i: (0, 0)
    return pl.pallas_call(
        _merge_kernel,
        grid=(n // tm,),
        in_specs=[pl.BlockSpec((tm, D_MODEL), row),
                  pl.BlockSpec((tm, oa.shape[1]), row), pl.BlockSpec((tm, ob.shape[1]), row),
                  pl.BlockSpec((tm, oc.shape[1]), row), pl.BlockSpec((tm, ox.shape[1]), row),
                  pl.BlockSpec((1, D_MODEL), full),
                  pl.BlockSpec(lw["wg"].shape, full, pipeline_mode=pl.Buffered(1)),
                  pl.BlockSpec(lw["wbr"].shape, full, pipeline_mode=pl.Buffered(1)),
                  pl.BlockSpec(lw["wo"].shape, full, pipeline_mode=pl.Buffered(1)), pl.BlockSpec((1, D_MODEL), full),
                  pl.BlockSpec(lw["wrh"].shape, full), pl.BlockSpec(lw["wrl"].shape, full)],
        out_specs=[pl.BlockSpec((tm, D_MODEL), row), pl.BlockSpec((tm, D_MODEL), row),
                   pl.BlockSpec((N_EXPERTS, tm), lambda i: (0, i))],
        out_shape=[jax.ShapeDtypeStruct((n, D_MODEL), jnp.float32),
                   jax.ShapeDtypeStruct((n, D_MODEL), MXU_DTYPE),
                   jax.ShapeDtypeStruct((N_EXPERTS, n), jnp.float32)],
        compiler_params=pltpu.CompilerParams(dimension_semantics=("parallel",), vmem_limit_bytes=VMEM_LIMIT),
        name="merge_router",
    )(x2, oa, ob, oc, ox, lw["g1"], lw["wg"], lw["wbr"], lw["wo"], lw["g2"], lw["wrh"], lw["wrl"])


ROW_TILE = (8, LANES)


def _expert_kernel(xs_ref, wgu_ref, wd_ref, y_ref):
    tc = xs_ref.shape[1]
    gu = _dot(xs_ref[0].reshape(tc, D_MODEL), wgu_ref[0])
    g, u = gu[:, :D_EXPERT], gu[:, D_EXPERT:]
    act = (g * jax.nn.sigmoid(g) * u).astype(MXU_DTYPE)
    y_ref[0] = _dot(act, wd_ref[0]).astype(y_ref.dtype).reshape((tc,) + ROW_TILE)


def _expert_ffn(xs, wgu, wd, cap, tc):
    nt = cap // tc
    return pl.pallas_call(
        _expert_kernel,
        grid=(N_EXPERTS, nt),
        in_specs=[pl.BlockSpec((1, tc) + ROW_TILE, lambda e, c: (e, c, 0, 0)),
                  pl.BlockSpec((1, D_MODEL, 2 * D_EXPERT), lambda e, c: (e, 0, 0)),
                  pl.BlockSpec((1, D_EXPERT, D_MODEL), lambda e, c: (e, 0, 0))],
        out_specs=pl.BlockSpec((1, tc) + ROW_TILE, lambda e, c: (e, c, 0, 0)),
        out_shape=jax.ShapeDtypeStruct((N_EXPERTS, cap) + ROW_TILE, MXU_DTYPE),
        compiler_params=pltpu.CompilerParams(dimension_semantics=("parallel", "parallel"),
                                             vmem_limit_bytes=VMEM_LIMIT),
        name="expert_ffn",
    )(xs, wgu, wd)


CHUNK = 32
MAX_BLOCKS = 128
ROUTE_CHUNK = 256


def _route_kernel(aff_ref, sel_ref, cb_ref, *, cap, n):
    bits = lax.bitcast_convert_type(aff_ref[...], jnp.int32)

    def search(b, thr):
        cand = thr | jnp.left_shift(jnp.int32(1), 30 - b)
        cnt = jnp.sum((bits >= cand).astype(jnp.int32), axis=1, keepdims=True)
        return jnp.where(cnt >= cap, cand, thr)

    thr = lax.fori_loop(0, 31, search, jnp.zeros((N_EXPERTS, 1), jnp.int32))
    n_gt = jnp.sum((bits > thr).astype(jnp.int32), axis=1, keepdims=True)
    need = (cap - n_gt).astype(jnp.float32)
    k = lax.broadcasted_iota(jnp.int32, (ROUTE_CHUNK, ROUTE_CHUNK), 0)
    j = lax.broadcasted_iota(jnp.int32, (ROUTE_CHUNK, ROUTE_CHUNK), 1)
    tri = jnp.where(k <= j, 1.0, 0.0).astype(MXU_DTYPE)

    def chunk(c, carry):
        cg, ce = carry
        off = pl.multiple_of(c * ROUTE_CHUNK, ROUTE_CHUNK)
        b = lax.bitcast_convert_type(aff_ref[:, pl.ds(off, ROUTE_CHUNK)], jnp.int32)
        gt = jnp.where(b > thr, 1.0, 0.0)
        eq = jnp.where(b == thr, 1.0, 0.0)
        pre = _dot(jnp.concatenate([gt, eq], axis=0).astype(MXU_DTYPE), tri)
        pg = pre[:N_EXPERTS] + cg
        pe = pre[N_EXPERTS:] + ce
        before = (pg - gt) + jnp.minimum(pe - eq, need)
        chosen = (gt > 0.0) | ((eq > 0.0) & ((pe - eq) < need))
        before_i = before.astype(jnp.int32)
        cb_ref[:, pl.ds(off, ROUTE_CHUNK)] = before_i
        sel_ref[:, pl.ds(off, ROUTE_CHUNK)] = jnp.where(chosen, before_i, -1)
        return pg[:, ROUTE_CHUNK - 1:ROUTE_CHUNK], pe[:, ROUTE_CHUNK - 1:ROUTE_CHUNK]

    zero = jnp.zeros((N_EXPERTS, 1), jnp.float32)
    lax.fori_loop(0, n // ROUTE_CHUNK, chunk, (zero, zero))


def _route(aff, cap):
    n = aff.shape[1]
    full = lambda i: (0, 0)
    return pl.pallas_call(
        functools.partial(_route_kernel, cap=cap, n=n),
        grid=(1,),
        in_specs=[pl.BlockSpec((N_EXPERTS, n), full)],
        out_specs=[pl.BlockSpec((N_EXPERTS, n), full), pl.BlockSpec((N_EXPERTS, n), full)],
        out_shape=[jax.ShapeDtypeStruct((N_EXPERTS, n), jnp.int32)] * 2,
        compiler_params=pltpu.CompilerParams(dimension_semantics=("arbitrary",), vmem_limit_bytes=VMEM_LIMIT),
        name="route_select",
    )(aff)


def _block_table(ptab_ref, i, tab_e, tab_base, tab_lo, slot, limit):
    def per_expert(e, nb):
        p0 = ptab_ref[e, i]
        nblk = jnp.right_shift(ptab_ref[e, i + 1] - p0 + (CHUNK - 1), 5)

        def per_block(kk, b):
            tab_e[slot, b] = e
            tab_lo[slot, b] = p0 + kk * CHUNK
            tab_base[slot, b] = jnp.minimum(p0 + kk * CHUNK, limit)
            return b + 1

        return lax.fori_loop(0, nblk, per_block, nb)

    nb = jnp.int32(0)
    for e in range(N_EXPERTS):
        nb = per_expert(e, nb)
    return nb


def _dispatch_kernel(ptab_ref, sel_ref, h_ref, xs_ref, oh_ref, res_ref, tab_e, tab_base, tab_lo, cnt_ref, sem, *,
                     cap, tm):
    i = pl.program_id(0)
    last = pl.num_programs(0) - 1

    def chunk_copy(b):
        e = tab_e[0, b]
        base = tab_base[0, b]
        per = tm // CHUNK
        src = res_ref.at[b // per, pl.ds(pl.multiple_of(lax.rem(b, per) * CHUNK, CHUNK), CHUNK)]
        return pltpu.make_async_copy(src, xs_ref.at[e, pl.ds(base, CHUNK)], sem.at[0])

    def wait_all(nb):
        def w(b, c):
            chunk_copy(b).wait()
            return c
        lax.fori_loop(0, nb, w, 0)

    @pl.when(i == 0)
    def _():
        oh_ref[...] = jnp.zeros(oh_ref.shape, oh_ref.dtype)
        cnt_ref[0] = 0
        res_ref[0] = jnp.zeros(res_ref.shape[1:], res_ref.dtype)
        pads = [pltpu.make_async_copy(res_ref.at[0, pl.ds(0, CHUNK)], xs_ref.at[e, pl.ds(cap, CHUNK)], sem.at[0])
                for e in range(N_EXPERTS)]
        for c in pads:
            c.start()
        for c in pads:
            c.wait()

    wait_all(cnt_ref[0])
    nb = _block_table(ptab_ref, i, tab_e, tab_base, tab_lo, 0, cap)

    def build(b, c):
        e = tab_e[0, b]
        slots = tab_base[0, b] + lax.broadcasted_iota(jnp.int32, (CHUNK, tm), 0)
        hit = sel_ref[pl.ds(e, 1), :] == slots
        oh_ref[pl.ds(pl.multiple_of(b * CHUNK, CHUNK), CHUNK), :] = jnp.where(hit, 1.0, 0.0).astype(oh_ref.dtype)
        return c

    lax.fori_loop(0, nb, build, 0)

    def group(gi, c):
        rows = pl.ds(pl.multiple_of(gi * tm, tm), tm)
        res_ref[gi] = _dot(oh_ref[rows, :], h_ref[...]).astype(res_ref.dtype).reshape((tm,) + ROW_TILE)
        return c

    lax.fori_loop(0, jnp.right_shift(nb * CHUNK + tm - 1, tm.bit_length() - 1), group, 0)

    def issue(b, c):
        chunk_copy(b).start()
        return c

    lax.fori_loop(0, nb, issue, 0)
    cnt_ref[0] = nb

    @pl.when(i == last)
    def _():
        wait_all(nb)
        cnt_ref[0] = 0


def _dispatch(ptab, sel, h2, cap, tm):
    n = h2.shape[0]
    grid_spec = pltpu.PrefetchScalarGridSpec(
        num_scalar_prefetch=1,
        grid=(n // tm,),
        in_specs=[pl.BlockSpec((N_EXPERTS, tm), lambda i, p: (0, i)),
                  pl.BlockSpec((tm, D_MODEL), lambda i, p: (i, 0))],
        out_specs=pl.BlockSpec(memory_space=pl.ANY),
        scratch_shapes=[pltpu.VMEM((MAX_BLOCKS * CHUNK, tm), MXU_DTYPE),
                        pltpu.VMEM((MAX_BLOCKS * CHUNK // tm, tm) + ROW_TILE, MXU_DTYPE),
                        pltpu.SMEM((1, MAX_BLOCKS), jnp.int32), pltpu.SMEM((1, MAX_BLOCKS), jnp.int32),
                        pltpu.SMEM((1, MAX_BLOCKS), jnp.int32),
                        pltpu.SMEM((1,), jnp.int32), pltpu.SemaphoreType.DMA((1,))],
    )
    return pl.pallas_call(
        functools.partial(_dispatch_kernel, cap=cap, tm=tm),
        grid_spec=grid_spec,
        out_shape=jax.ShapeDtypeStruct((N_EXPERTS, cap + CHUNK) + ROW_TILE, MXU_DTYPE),
        compiler_params=pltpu.CompilerParams(dimension_semantics=("arbitrary",), vmem_limit_bytes=VMEM_LIMIT),
        name="moe_dispatch",
    )(ptab, sel, h2)


def _combine_kernel(ptab_ref, sel_ref, aff_ref, y_ref, ye_ref, o_ref, stage_ref, wt_ref, tab_e, tab_base, tab_lo,
                    cnt_ref, sem, *, cap, tm):
    i = pl.program_id(0)
    nt = pl.num_programs(0)
    slot = lax.rem(i, 2)

    def chunk_copy(s, b):
        e = tab_e[s, b]
        base = tab_base[s, b]
        dst = stage_ref.at[s, pl.ds(pl.multiple_of(b * CHUNK, CHUNK), CHUNK)]
        return pltpu.make_async_copy(ye_ref.at[e, pl.ds(base, CHUNK)], dst, sem.at[s])

    def fetch(tile, s):
        nb = _block_table(ptab_ref, tile, tab_e, tab_base, tab_lo, s, cap - CHUNK)
        cnt_ref[s] = nb

        def issue(b, c):
            chunk_copy(s, b).start()
            return c

        lax.fori_loop(0, nb, issue, 0)

    @pl.when(i == 0)
    def _():
        stage_ref[...] = jnp.zeros(stage_ref.shape, stage_ref.dtype)
        fetch(i, slot)

    @pl.when(i + 1 < nt)
    def _():
        fetch(i + 1, 1 - slot)

    nb = cnt_ref[slot]

    def wait(b, c):
        chunk_copy(slot, b).wait()
        return c

    lax.fori_loop(0, nb, wait, 0)
    ngroups = jnp.right_shift(nb * CHUNK + tm - 1, tm.bit_length() - 1)

    def build(b, c):
        e = tab_e[slot, b]
        slots = tab_base[slot, b] + lax.broadcasted_iota(jnp.int32, (CHUNK, tm), 0)
        hit = (sel_ref[pl.ds(e, 1), :] == slots) & (slots >= tab_lo[slot, b])
        w = jnp.where(hit, aff_ref[pl.ds(e, 1), :], 0.0)
        wt_ref[pl.ds(pl.multiple_of(b * CHUNK, CHUNK), CHUNK), :] = w.astype(wt_ref.dtype)
        return c

    def clear(b, c):
        wt_ref[pl.ds(pl.multiple_of(b * CHUNK, CHUNK), CHUNK), :] = jnp.zeros((CHUNK, tm), wt_ref.dtype)
        return c

    lax.fori_loop(0, nb, build, 0)
    lax.fori_loop(nb, ngroups * (tm // CHUNK), clear, 0)
    o_ref[...] = y_ref[...]

    def group(gi, c):
        rows = pl.ds(pl.multiple_of(gi * tm, tm), tm)
        o_ref[...] += lax.dot_general(wt_ref[rows, :], stage_ref[slot, rows].reshape(tm, D_MODEL),
                                      (((0,), (0,)), ((), ())), preferred_element_type=jnp.float32)
        return c

    lax.fori_loop(0, ngroups, group, 0)


def _combine(ptab, sel, aff, y, ye, cap, tm):
    n = y.shape[0]
    grid_spec = pltpu.PrefetchScalarGridSpec(
        num_scalar_prefetch=1,
        grid=(n // tm,),
        in_specs=[pl.BlockSpec((N_EXPERTS, tm), lambda i, p: (0, i)),
                  pl.BlockSpec((N_EXPERTS, tm), lambda i, p: (0, i)),
                  pl.BlockSpec((tm, D_MODEL), lambda i, p: (i, 0)),
                  pl.BlockSpec(memory_space=pl.ANY)],
        out_specs=pl.BlockSpec((tm, D_MODEL), lambda i, p: (i, 0)),
        scratch_shapes=[pltpu.VMEM((2, MAX_BLOCKS * CHUNK) + ROW_TILE, MXU_DTYPE),
                        pltpu.VMEM((MAX_BLOCKS * CHUNK, tm), MXU_DTYPE),
                        pltpu.SMEM((2, MAX_BLOCKS), jnp.int32), pltpu.SMEM((2, MAX_BLOCKS), jnp.int32),
                        pltpu.SMEM((2, MAX_BLOCKS), jnp.int32),
                        pltpu.SMEM((2,), jnp.int32), pltpu.SemaphoreType.DMA((2,))],
    )
    return pl.pallas_call(
        functools.partial(_combine_kernel, cap=cap, tm=tm),
        grid_spec=grid_spec,
        out_shape=jax.ShapeDtypeStruct((n, D_MODEL), jnp.float32),
        compiler_params=pltpu.CompilerParams(dimension_semantics=("arbitrary",), vmem_limit_bytes=VMEM_LIMIT),
        name="moe_combine",
    )(ptab, sel, aff, y, ye)


def _rope_tables(seq):
    pos = jnp.arange(seq, dtype=jnp.int32)
    inv = jnp.power(ROPE_THETA, -jnp.arange(0, 32, 2, dtype=jnp.float32) / 32.0)
    ang_row = (pos // GRID_W).astype(jnp.float32)[:, None] * inv[None, :]
    ang_col = (pos % GRID_W).astype(jnp.float32)[:, None] * inv[None, :]
    ang_seq = pos.astype(jnp.float32)[:, None] * inv[None, :]
    sign = jnp.concatenate([-jnp.ones((16,), jnp.float32), jnp.ones((16,), jnp.float32)])

    def pair(ang):
        return jnp.tile(jnp.cos(ang), (1, 2)), jnp.tile(jnp.sin(ang), (1, 2)) * sign[None, :]

    cr, sr = pair(ang_row)
    cc, sc = pair(ang_col)
    cs, ss = pair(ang_seq)
    one = jnp.ones((seq, 32), jnp.float32)
    zero = jnp.zeros((seq, 32), jnp.float32)
    return {
        "cosa": jnp.concatenate([cr, cc, cr, cc], axis=1),
        "sina": jnp.concatenate([sr, sc, sr, sc], axis=1),
        "cosb": jnp.concatenate([one, one, cs, one], axis=1),
        "sinb": jnp.concatenate([zero, zero, ss, zero], axis=1),
    }


def _prep_layer(l, p):
    f32 = jnp.float32
    w_in = p["w_in"][l]
    a0, b0, c0, x0, g0 = 0, 768, 1440, 2208, 2464

    kr = jnp.zeros((D_MODEL, LANES), f32).at[:, 64:96].set(w_in[:, b0 + 640:b0 + 672])
    w1 = jnp.concatenate([
        w_in[:, a0:a0 + 768],
        w_in[:, b0:b0 + 384], w_in[:, b0 + 384:b0 + 640],
        w_in[:, c0:c0 + 768],
        w_in[:, x0:x0 + 256], kr], axis=1)
    assert w1.shape[1] == _C_END

    wqb = jnp.zeros((B_Q_RANK, B_HEADS, LANES), f32).at[:, :, :B_QK].set(
        p["b_w_q_b"][l].reshape(B_Q_RANK, B_HEADS, B_QK)).reshape(B_Q_RANK, B_HEADS * LANES)
    wkv = p["b_w_kv_b"][l].reshape(B_KV_RANK, B_HEADS, 2 * B_NOPE)
    wkb = jnp.zeros((B_KV_RANK, B_HEADS, LANES), f32).at[:, :, :B_NOPE].set(
        wkv[:, :, :B_NOPE]).reshape(B_KV_RANK, B_HEADS * LANES)
    wvb = wkv[:, :, B_NOPE:].reshape(B_KV_RANK, B_HEADS * B_NOPE)

    def g64(g, scale=1.0):
        return jnp.tile(g * scale, 2)

    def g96(g, scale=1.0):
        return jnp.concatenate([g * scale, jnp.zeros((LANES - B_QK,), f32)])

    gains = jnp.stack([
        g64(p["a_q_norm"][l], HEAD ** -0.5 * LOG2E), g64(p["a_k_norm"][l]),
        g96(p["b_q_norm"][l], B_QK ** -0.5 * LOG2E), g96(p["b_k_norm"][l]),
        g64(p["c_q_norm"][l], HEAD ** -0.5 * LOG2E), g64(p["c_k_norm"][l]),
        g64(p["x_q_norm"][l], HEAD ** -0.5 * LOG2E), jnp.zeros((LANES,), f32)])

    wr = p["w_router"][l].T
    wrh = wr.astype(MXU_DTYPE)
    wrl = (wr - wrh.astype(f32)).astype(MXU_DTYPE)
    bf = lambda a: a.astype(MXU_DTYPE)
    return {
        "g1": p["norm1_g"][l][None, :], "w1": bf(w1), "wqb": bf(wqb), "wkb": bf(wkb), "wvb": bf(wvb),
        "gains": gains, "gbq": p["b_q_a_norm"][l][None, :], "gbkv": p["b_kv_a_norm"][l][None, :],
        "gmem": p["mem_norm_g"][l][None, :], "wmem": bf(p["w_mem_kv"][l]), "gxk": g64(p["x_k_norm"][l])[None, :],
        "sink": p["c_sink"][l],
        "wg": bf(w_in[:, g0:]), "wbr": bf(p["w_branch"][l]), "wo": bf(p["w_out"][l]),
        "g2": p["norm2_g"][l][None, :], "wrh": wrh, "wrl": wrl,
        "wgu": bf(p["w_gate_up"][l]), "wd": bf(p["w_down"][l]),
    }


def _trunk(x, mem, layers):
    batch, seq, _ = x.shape
    n = batch * seq
    tm = min(256, seq)
    tp = min(512, seq)
    cap = EC_FACTOR * n // N_EXPERTS
    tc = min(256, cap)
    tabs = _rope_tables(seq)
    x2 = x.reshape(n, D_MODEL)
    mem2 = mem.reshape(batch * N_MEM, D_MODEL)
    for lw in layers:
        qa, ka, va, qb, kb, vb, qc, kc, vc, qx = _in_projection(x2, lw, tabs, seq, tp)
        kx, vx = _memory_kv(mem2, lw)
        oa = _flash_attention(qa, ka, va, mode="gqa", batch=batch, seq=seq, kv_len=seq, tq=TQ_A, tk=TK_A)
        ob = _flash_attention(qb, kb, vb, mode="mha", batch=batch, seq=seq, kv_len=seq, tq=TQ_B, tk=TK_B)
        oc = _window_attention(qc, kc, vc, lw["sink"], batch=batch, seq=seq, tq=min(256, seq - 2 * C_WINDOW))
        ox = _flash_attention(qx, kx, vx, mode="cross", batch=batch, seq=seq, kv_len=N_MEM, tq=TQ_X, tk=N_MEM)
        y, h2, aff = _merge(x2, oa, ob, oc, ox, lw, tp)
        sel, before = _route(aff, cap)
        ptab = jnp.concatenate([before[:, ::tm], jnp.full((N_EXPERTS, 1), cap, jnp.int32)], axis=1)
        xs = _dispatch(ptab, sel, h2, cap, tm)
        ye = _expert_ffn(xs, lw["wgu"], lw["wd"], cap, tc)
        x2 = _combine(ptab, sel, aff, y, ye, cap, tm)
    return x2.reshape(batch, seq, D_MODEL)


def kernel(x_prompt, x_sample, mem_prompt, mem_sample, norm1_g, w_in, a_q_norm, a_k_norm, b_q_a_norm,
           b_w_q_b, b_kv_a_norm, b_w_kv_b, b_q_norm, b_k_norm, c_q_norm, c_k_norm, c_sink, x_q_norm,
           x_k_norm, mem_norm_g, w_mem_kv, w_branch, w_out, norm2_g, w_router, w_gate_up, w_down):
    p = dict(norm1_g=norm1_g, w_in=w_in, a_q_norm=a_q_norm, a_k_norm=a_k_norm, b_q_a_norm=b_q_a_norm,
             b_w_q_b=b_w_q_b, b_kv_a_norm=b_kv_a_norm, b_w_kv_b=b_w_kv_b, b_q_norm=b_q_norm, b_k_norm=b_k_norm,
             c_q_norm=c_q_norm, c_k_norm=c_k_norm, c_sink=c_sink, x_q_norm=x_q_norm, x_k_norm=x_k_norm,
             mem_norm_g=mem_norm_g, w_mem_kv=w_mem_kv, w_branch=w_branch, w_out=w_out, norm2_g=norm2_g,
             w_router=w_router, w_gate_up=w_gate_up, w_down=w_down)
    layers = [_prep_layer(l, p) for l in range(w_in.shape[0])]
    return (_trunk(x_prompt, mem_prompt, layers), _trunk(x_sample, mem_sample, layers))
```

```python
import functools

import jax
import jax.numpy as jnp
from jax import lax
from jax.experimental import pallas as pl
from jax.experimental.pallas import tpu as pltpu

D_MODEL = 1024
GRID_W = 64
N_MEM = 256
EPS = 1e-6
ROPE_THETA = 10000.0
NEG_INF = -1e30
HEAD = 64
A_KV_HEADS = 2
B_HEADS = 8
B_Q_RANK = 384
B_KV_RANK = 256
B_NOPE = 64
B_ROPE = 32
B_QK = B_NOPE + B_ROPE
C_WINDOW = 128
X_HEADS = 4
N_EXPERTS = 16
EC_FACTOR = 2
D_EXPERT = 1024
LANES = 128

MXU_DTYPE = jnp.bfloat16
VMEM_LIMIT = 56 * 1024 * 1024
LOG2E = 1.4426950408889634
TQ_A, TK_A = 1024, 512
TQ_B, TK_B = 2048, 512
TQ_X = 1024

_C_AQ, _C_AK, _C_AV = 0, 512, 640
_C_BCQ, _C_BCKV = 768, 1152
_C_CQ, _C_CK, _C_CV = 1408, 1920, 2048
_C_XQ, _C_KR, _C_END = 2176, 2432, 2560
V_ROWS = 80


def _dot(a, b):
    return jnp.dot(a, b, preferred_element_type=jnp.float32)


def _dot_nt(a, b):
    return lax.dot_general(a, b, (((1,), (1,)), ((), ())), preferred_element_type=jnp.float32)


def _rms(x, g):
    return x * lax.rsqrt(jnp.mean(x * x, axis=-1, keepdims=True) + EPS) * g


def _block_ones(head_w):
    w = 2 * LANES
    row = lax.broadcasted_iota(jnp.int32, (w, w), 0) // head_w
    col = lax.broadcasted_iota(jnp.int32, (w, w), 1) // head_w
    return jnp.where(row == col, 1.0, 0.0).astype(MXU_DTYPE)


def _head_norm(z, gain, ones, n_valid):
    width = z.shape[1]
    outs = []
    c = 0
    while c < width:
        w = min(2 * LANES, width - c)
        blk = z[:, c:c + w]
        ss = _dot((blk * blk).astype(MXU_DTYPE), ones[:w, :w])
        g = gain if w == LANES else jnp.concatenate([gain, gain], axis=1)
        normed = blk * lax.rsqrt(ss * (1.0 / n_valid) + EPS) * g
        outs += [normed[:, i * LANES:(i + 1) * LANES] for i in range(w // LANES)]
        c += w
    return outs


def _rope(blocks, cos, sin_signed):
    lane = lax.broadcasted_iota(jnp.int32, (1, LANES), 1)
    first = (lane % 32) < 16
    outs = []
    for blk in blocks:
        up = pltpu.roll(blk, LANES - 16, 1)
        dn = pltpu.roll(blk, 16, 1)
        outs.append(blk * cos + jnp.where(first, up, dn) * sin_signed)
    return outs


def _cat(blocks, dtype):
    return jnp.concatenate(blocks, axis=1).astype(dtype)


def _transposed_v(v, dtype):
    vt = v.T
    ones = jnp.ones((V_ROWS - HEAD, v.shape[0]), jnp.float32)
    parts = []
    for h in range(v.shape[1] // HEAD):
        parts += [vt[h * HEAD:(h + 1) * HEAD, :], ones]
    return jnp.concatenate(parts, axis=0).astype(dtype)


def _inproj_kernel(x_ref, g1_ref, w1_ref, wqb_ref, wkb_ref, wvb_ref, gains_ref, gbq_ref, gbkv_ref,
                   cosa_ref, sina_ref, cosb_ref, sinb_ref,
                   qa_ref, ka_ref, va_ref, qb_ref, kb_ref, vb_ref, qc_ref, kc_ref, vc_ref, qx_ref):
    x = x_ref[...]
    hb = _rms(x, g1_ref[...]).astype(MXU_DTYPE)
    z = _dot(hb, w1_ref[...])
    gains = gains_ref[...]
    cosa, sina = cosa_ref[...], sina_ref[...]
    cosb, sinb = cosb_ref[...], sinb_ref[...]
    od = qa_ref.dtype
    ones64, ones128 = _block_ones(HEAD), _block_ones(LANES)

    qa = _rope(_head_norm(z[:, _C_AQ:_C_AK], gains[0:1], ones64, HEAD), cosa, sina)
    qa_ref[...] = _cat(qa, od)
    ka = _rope(_head_norm(z[:, _C_AK:_C_AV], gains[1:2], ones64, HEAD), cosa, sina)
    ka_ref[...] = ka[0].astype(od)
    va_ref[...] = _transposed_v(z[:, _C_AV:_C_BCQ], od)

    cq = _rms(z[:, _C_BCQ:_C_BCKV], gbq_ref[...]).astype(MXU_DTYPE)
    ckv = _rms(z[:, _C_BCKV:_C_CQ], gbkv_ref[...]).astype(MXU_DTYPE)
    q8 = _dot(cq, wqb_ref[...])
    k8 = _dot(ckv, wkb_ref[...])
    kr = z[:, _C_KR:_C_END]
    k8 = k8 + jnp.concatenate([kr] * B_HEADS, axis=1)
    qb_ref[...] = _cat(_rope(_head_norm(q8, gains[2:3], ones128, B_QK), cosb, sinb), od)
    kb_ref[...] = _cat(_rope(_head_norm(k8, gains[3:4], ones128, B_QK), cosb, sinb), od)
    vb_ref[...] = _transposed_v(_dot(ckv, wvb_ref[...]), od)

    qc_ref[...] = _cat(_head_norm(z[:, _C_CQ:_C_CK], gains[4:5], ones64, HEAD), od)
    kc_ref[...] = _head_norm(z[:, _C_CK:_C_CV], gains[5:6], ones64, HEAD)[0].astype(od)
    vc_ref[...] = _transposed_v(z[:, _C_CV:_C_XQ], od)

    qx_ref[...] = _cat(_head_norm(z[:, _C_XQ:_C_KR], gains[6:7], ones64, HEAD), od)


def _in_projection(x2, lw, tabs, seq, tm):
    n = x2.shape[0]
    nt = seq // tm
    row = lambda i: (i, 0)
    full = lambda i: (0, 0)
    tab = lambda i: (i % nt, 0)
    col = lambda i: (0, i)
    outs_def = [(512, False), (LANES, False), (A_KV_HEADS * V_ROWS, True), (1024, False), (1024, False),
                (B_HEADS * V_ROWS, True), (512, False), (LANES, False), (A_KV_HEADS * V_ROWS, True), (256, False)]
    out_specs = [pl.BlockSpec((w, tm), col) if t else pl.BlockSpec((tm, w), row) for w, t in outs_def]
    out_shape = [jax.ShapeDtypeStruct((w, n) if t else (n, w), MXU_DTYPE) for w, t in outs_def]
    outs = pl.pallas_call(
        _inproj_kernel,
        grid=(n // tm,),
        in_specs=[
            pl.BlockSpec((tm, D_MODEL), row),
            pl.BlockSpec((1, D_MODEL), full),
            pl.BlockSpec(lw["w1"].shape, full),
            pl.BlockSpec(lw["wqb"].shape, full),
            pl.BlockSpec(lw["wkb"].shape, full),
            pl.BlockSpec(lw["wvb"].shape, full),
            pl.BlockSpec(lw["gains"].shape, full),
            pl.BlockSpec((1, B_Q_RANK), full),
            pl.BlockSpec((1, B_KV_RANK), full),
            pl.BlockSpec((tm, LANES), tab),
            pl.BlockSpec((tm, LANES), tab),
            pl.BlockSpec((tm, LANES), tab),
            pl.BlockSpec((tm, LANES), tab),
        ],
        out_specs=out_specs,
        out_shape=out_shape,
        compiler_params=pltpu.CompilerParams(dimension_semantics=("parallel",), vmem_limit_bytes=VMEM_LIMIT),
        name="in_projection",
    )(x2, lw["g1"], lw["w1"], lw["wqb"], lw["wkb"], lw["wvb"], lw["gains"], lw["gbq"], lw["gbkv"],
      tabs["cosa"], tabs["sina"], tabs["cosb"], tabs["sinb"])
    return outs


def _memkv_kernel(mem_ref, g_ref, w_ref, gain_ref, k_ref, v_ref):
    m = _rms(mem_ref[...], g_ref[...]).astype(MXU_DTYPE)
    kv = _dot(m, w_ref[...])
    half = kv.shape[1] // 2
    k_ref[...] = _cat(_head_norm(kv[:, :half], gain_ref[...], _block_ones(HEAD), HEAD), k_ref.dtype)
    v_ref[...] = _transposed_v(kv[:, half:], v_ref.dtype)


def _memory_kv(mem2, lw):
    n = mem2.shape[0]
    tm = N_MEM
    xq = X_HEADS * HEAD
    row = lambda i: (i, 0)
    full = lambda i: (0, 0)
    return pl.pallas_call(
        _memkv_kernel,
        grid=(n // tm,),
        in_specs=[pl.BlockSpec((tm, D_MODEL), row), pl.BlockSpec((1, D_MODEL), full),
                  pl.BlockSpec((D_MODEL, 2 * xq), full), pl.BlockSpec((1, LANES), full)],
        out_specs=[pl.BlockSpec((tm, xq), row), pl.BlockSpec((X_HEADS * V_ROWS, tm), lambda i: (0, i))],
        out_shape=[jax.ShapeDtypeStruct((n, xq), MXU_DTYPE),
                   jax.ShapeDtypeStruct((X_HEADS * V_ROWS, n), MXU_DTYPE)],
        compiler_params=pltpu.CompilerParams(dimension_semantics=("parallel",)),
        name="memory_kv",
    )(mem2, lw["gmem"], lw["wmem"], lw["gxk"])


def _flash_kernel(q_ref, k_ref, v_ref, o_ref, qs_ref, s_ref, m_ref, acc_ref, *, mode, nh, tq, tk, n_kv):
    g = pl.program_id(1)
    qt = q_ref[...].astype(jnp.float32).T
    kdim = qs_ref.shape[0]
    for i in range(nh):
        if mode == "gqa":
            blk = qt[i * HEAD:(i + 1) * HEAD, :]
            half = lax.broadcasted_iota(jnp.int32, (kdim, tq), 0) // HEAD
            op = jnp.where(half == g, jnp.concatenate([blk, blk], axis=0), 0.0)
        elif mode == "mha":
            op = qt[i * LANES:(i + 1) * LANES, :]
        else:
            head = lax.broadcasted_iota(jnp.int32, (kdim, tq), 0) // HEAD
            op = jnp.where(head == i, qt, 0.0)
        qs_ref[:, i * tq:(i + 1) * tq] = op.astype(qs_ref.dtype)
    m_ref[...] = jnp.full(m_ref.shape, -jnp.inf, jnp.float32)
    acc_ref[...] = jnp.zeros(acc_ref.shape, jnp.float32)

    def score(j, slot):
        off = pl.multiple_of(j * tk, tk)
        if mode == "mha":
            for i in range(nh):
                cols = slice(i * tq, (i + 1) * tq)
                s_ref[slot, :, cols] = _dot(k_ref[pl.ds(off, tk), i * LANES:(i + 1) * LANES], qs_ref[:, cols])
        else:
            s_ref[slot] = _dot(k_ref[pl.ds(off, tk), :], qs_ref[...])

    def consume(j, slot):
        off = pl.multiple_of(j * tk, tk)
        s = s_ref[slot]
        m_prev = m_ref[0:1, :]
        m_new = jnp.maximum(m_prev, jnp.max(s, axis=0, keepdims=True))
        alpha = jnp.exp2(m_prev - m_new)
        p = jnp.exp2(s - m_new).astype(v_ref.dtype)
        if mode == "gqa":
            pv = _dot(v_ref[:, pl.ds(off, tk)], p)
        else:
            pv = jnp.concatenate(
                [_dot(v_ref[i * V_ROWS:(i + 1) * V_ROWS, pl.ds(off, tk)], p[:, i * tq:(i + 1) * tq])
                 for i in range(nh)], axis=1)
        acc_ref[...] = alpha * acc_ref[...] + pv
        m_ref[0:1, :] = m_new

    score(0, 0)
    if n_kv > 1:
        def pair(jj, carry):
            score(2 * jj + 1, 1)
            consume(2 * jj, 0)
            score(2 * jj + 2, 0)
            consume(2 * jj + 1, 1)
            return carry

        lax.fori_loop(0, n_kv // 2 - 1, pair, 0)
        score(n_kv - 1, 1)
        consume(n_kv - 2, 0)
        consume(n_kv - 1, 1)
    else:
        consume(0, 0)
    a = acc_ref[...]
    ot = a[:HEAD, :] / a[HEAD:HEAD + 1, :]
    o_ref[...] = jnp.concatenate([ot[:, i * tq:(i + 1) * tq] for i in range(nh)], axis=0).T.astype(o_ref.dtype)


def _flash_attention(q, k, vt, *, mode, batch, seq, kv_len, tq, tk):
    kw = 256
    groups = q.shape[1] // kw
    nh = {"gqa": 4, "mha": 2, "cross": 4}[mode]
    kcols = {"gqa": LANES, "mha": kw, "cross": kw}[mode]
    kdim = {"gqa": LANES, "mha": LANES, "cross": kw}[mode]
    vrows = V_ROWS if mode == "gqa" else nh * V_ROWS
    k_map = (lambda b, g, i: (b, 0)) if mode == "gqa" else (lambda b, g, i: (b, g))
    tq = min(tq, seq)
    tk = min(tk, kv_len)
    nq = seq // tq
    ow = nh * HEAD
    kern = functools.partial(_flash_kernel, mode=mode, nh=nh, tq=tq, tk=tk, n_kv=kv_len // tk)
    return pl.pallas_call(
        kern,
        grid=(batch, groups, nq),
        in_specs=[pl.BlockSpec((tq, kw), lambda b, g, i: (b * nq + i, g)),
                  pl.BlockSpec((kv_len, kcols), k_map),
                  pl.BlockSpec((vrows, kv_len), lambda b, g, i: (g, b))],
        out_specs=pl.BlockSpec((tq, ow), lambda b, g, i: (b * nq + i, g)),
        out_shape=jax.ShapeDtypeStruct((batch * seq, groups * ow), MXU_DTYPE),
        scratch_shapes=[pltpu.VMEM((kdim, nh * tq), MXU_DTYPE), pltpu.VMEM((2, tk, nh * tq), jnp.float32),
                        pltpu.VMEM((8, nh * tq), jnp.float32), pltpu.VMEM((V_ROWS, nh * tq), jnp.float32)],
        compiler_params=pltpu.CompilerParams(dimension_semantics=("parallel", "parallel", "parallel"),
                                             vmem_limit_bytes=VMEM_LIMIT),
        name=f"flash_attention_{mode}",
    )(q, k, vt)


def _window_kernel(sink_ref, q_ref, k_ref, v_ref, o_ref, *, tq, seq):
    g = pl.program_id(1)
    j = pl.program_id(2)
    nh = 4
    win = tq + 2 * C_WINDOW
    start = pl.multiple_of(jnp.clip(j * tq - C_WINDOW, 0, seq - win), C_WINDOW)
    k = k_ref[pl.ds(start, win), :]
    vt = v_ref[:, pl.ds(start, win)]
    qt = q_ref[...].astype(jnp.float32).T
    half = lax.broadcasted_iota(jnp.int32, (2 * HEAD, tq), 0) // HEAD
    ops = []
    for i in range(nh):
        blk = qt[i * HEAD:(i + 1) * HEAD, :]
        ops.append(jnp.where(half == g, jnp.concatenate([blk, blk], axis=0), 0.0))
    s = _dot(k, jnp.concatenate(ops, axis=1).astype(k.dtype))
    spos = start + lax.broadcasted_iota(jnp.int32, (win, tq), 0)
    tpos = j * tq + lax.broadcasted_iota(jnp.int32, (win, tq), 1)
    rel = jnp.abs(spos - tpos)
    dist = jnp.where(rel <= C_WINDOW, rel.astype(jnp.float32), -NEG_INF)
    ps, sinks = [], []
    for i in range(nh):
        slope = jnp.where(g == 0, 2.0 ** -(i + 1), 2.0 ** -(i + 5)) * LOG2E
        si = s[:, i * tq:(i + 1) * tq] - slope * dist
        sink = sink_ref[nh * g + i] * LOG2E
        m = jnp.maximum(jnp.max(si, axis=0, keepdims=True), sink)
        ps.append(jnp.exp2(si - m).astype(vt.dtype))
        sinks.append(jnp.exp2(sink - m))
    acc = _dot(vt, jnp.concatenate(ps, axis=1))
    ot = acc[:HEAD, :] / (acc[HEAD:HEAD + 1, :] + jnp.concatenate(sinks, axis=1))
    o_ref[...] = jnp.concatenate([ot[:, i * tq:(i + 1) * tq] for i in range(nh)], axis=0).T.astype(o_ref.dtype)


def _window_attention(q, k, v, sink, *, batch, seq, tq):
    kw = 256
    groups = q.shape[1] // kw
    nq = seq // tq
    kern = functools.partial(_window_kernel, tq=tq, seq=seq)
    return pl.pallas_call(
        kern,
        grid=(batch, groups, nq),
        in_specs=[pl.BlockSpec(memory_space=pltpu.SMEM),
                  pl.BlockSpec((tq, kw), lambda b, g, i: (b * nq + i, g)),
                  pl.BlockSpec((seq, LANES), lambda b, g, i: (b, 0)),
                  pl.BlockSpec((V_ROWS, seq), lambda b, g, i: (g, b))],
        out_specs=pl.BlockSpec((tq, kw), lambda b, g, i: (b * nq + i, g)),
        out_shape=jax.ShapeDtypeStruct((batch * seq, groups * kw), MXU_DTYPE),
        compiler_params=pltpu.CompilerParams(dimension_semantics=("parallel", "parallel", "parallel"),
                                             vmem_limit_bytes=VMEM_LIMIT),
        name="window_attention",
    )(sink, q, k, v)


def _merge_kernel(x_ref, oa_ref, ob_ref, oc_ref, ox_ref, g1_ref, wg_ref, wbr_ref, wo_ref, g2_ref,
                  wrh_ref, wrl_ref, y_ref, h2_ref, aff_ref):
    x = x_ref[...]
    hb = _rms(x, g1_ref[...]).astype(MXU_DTYPE)
    merged = jnp.zeros(x.shape, jnp.float32)
    r0 = 0
    for i, o_ref in enumerate((oa_ref, ob_ref, oc_ref, ox_ref)):
        r1 = r0 + o_ref.shape[1]
        gate = jax.nn.sigmoid(_dot(hb, wg_ref[:, i * D_MODEL:(i + 1) * D_MODEL]))
        merged = merged + gate * _dot(o_ref[...], wbr_ref[r0:r1, :])
        r0 = r1
    y = x + _dot(merged.astype(MXU_DTYPE), wo_ref[...])
    y_ref[...] = y
    h2 = _rms(y, g2_ref[...])
    h2_hi = h2.astype(MXU_DTYPE)
    h2_lo = (h2 - h2_hi.astype(jnp.float32)).astype(MXU_DTYPE)
    h2_ref[...] = h2_hi
    wrh, wrl = wrh_ref[...], wrl_ref[...]
    logits = _dot_nt(wrh, h2_hi) + _dot_nt(wrh, h2_lo) + _dot_nt(wrl, h2_hi)
    e = jnp.exp(logits - jnp.max(logits, axis=0, keepdims=True))
    aff_ref[...] = e / jnp.sum(e, axis=0, keepdims=True)


def _merge(x2, oa, ob, oc, ox, lw, tm):
    n = x2.shape[0]
    row = lambda i: (i, 0)
    full = lambda i: (0, 0)
    return pl.pallas_call(
        _merge_kernel,
        grid=(n // tm,),
        in_specs=[pl.BlockSpec((tm, D_MODEL), row),
                  pl.BlockSpec((tm, oa.shape[1]), row), pl.BlockSpec((tm, ob.shape[1]), row),
                  pl.BlockSpec((tm, oc.shape[1]), row), pl.BlockSpec((tm, ox.shape[1]), row),
                  pl.BlockSpec((1, D_MODEL), full),
                  pl.BlockSpec(lw["wg"].shape, full, pipeline_mode=pl.Buffered(1)),
                  pl.BlockSpec(lw["wbr"].shape, full, pipeline_mode=pl.Buffered(1)),
                  pl.BlockSpec(lw["wo"].shape, full, pipeline_mode=pl.Buffered(1)), pl.BlockSpec((1, D_MODEL), full),
                  pl.BlockSpec(lw["wrh"].shape, full), pl.BlockSpec(lw["wrl"].shape, full)],
        out_specs=[pl.BlockSpec((tm, D_MODEL), row), pl.BlockSpec((tm, D_MODEL), row),
                   pl.BlockSpec((N_EXPERTS, tm), lambda i: (0, i))],
        out_shape=[jax.ShapeDtypeStruct((n, D_MODEL), jnp.float32),
                   jax.ShapeDtypeStruct((n, D_MODEL), MXU_DTYPE),
                   jax.ShapeDtypeStruct((N_EXPERTS, n), jnp.float32)],
        compiler_params=pltpu.CompilerParams(dimension_semantics=("parallel",), vmem_limit_bytes=VMEM_LIMIT),
        name="merge_router",
    )(x2, oa, ob, oc, ox, lw["g1"], lw["wg"], lw["wbr"], lw["wo"], lw["g2"], lw["wrh"], lw["wrl"])


ROW_TILE = (8, LANES)


def _expert_kernel(xs_ref, wgu_ref, wd_ref, y_ref):
    tc = xs_ref.shape[1]
    gu = _dot(xs_ref[0].reshape(tc, D_MODEL), wgu_ref[0])
    g, u = gu[:, :D_EXPERT], gu[:, D_EXPERT:]
    act = (g * jax.nn.sigmoid(g) * u).astype(MXU_DTYPE)
    y_ref[0] = _dot(act, wd_ref[0]).astype(y_ref.dtype).reshape((tc,) + ROW_TILE)


def _expert_ffn(xs, wgu, wd, cap, tc):
    nt = cap // tc
    return pl.pallas_call(
        _expert_kernel,
        grid=(N_EXPERTS, nt),
        in_specs=[pl.BlockSpec((1, tc) + ROW_TILE, lambda e, c: (e, c, 0, 0)),
                  pl.BlockSpec((1, D_MODEL, 2 * D_EXPERT), lambda e, c: (e, 0, 0)),
                  pl.BlockSpec((1, D_EXPERT, D_MODEL), lambda e, c: (e, 0, 0))],
        out_specs=pl.BlockSpec((1, tc) + ROW_TILE, lambda e, c: (e, c, 0, 0)),
        out_shape=jax.ShapeDtypeStruct((N_EXPERTS, cap) + ROW_TILE, MXU_DTYPE),
        compiler_params=pltpu.CompilerParams(dimension_semantics=("parallel", "parallel"),
                                             vmem_limit_bytes=VMEM_LIMIT),
        name="expert_ffn",
    )(xs, wgu, wd)


CHUNK = 32
MAX_BLOCKS = 128
ROUTE_CHUNK = 256


def _route_kernel(aff_ref, sel_ref, cb_ref, *, cap, n):
    bits = lax.bitcast_convert_type(aff_ref[...], jnp.int32)

    def search(b, thr):
        cand = thr | jnp.left_shift(jnp.int32(1), 30 - b)
        cnt = jnp.sum((bits >= cand).astype(jnp.int32), axis=1, keepdims=True)
        return jnp.where(cnt >= cap, cand, thr)

    thr = lax.fori_loop(0, 31, search, jnp.zeros((N_EXPERTS, 1), jnp.int32))
    n_gt = jnp.sum((bits > thr).astype(jnp.int32), axis=1, keepdims=True)
    need = (cap - n_gt).astype(jnp.float32)
    k = lax.broadcasted_iota(jnp.int32, (ROUTE_CHUNK, ROUTE_CHUNK), 0)
    j = lax.broadcasted_iota(jnp.int32, (ROUTE_CHUNK, ROUTE_CHUNK), 1)
    tri = jnp.where(k <= j, 1.0, 0.0).astype(MXU_DTYPE)

    def chunk(c, carry):
        cg, ce = carry
        off = pl.multiple_of(c * ROUTE_CHUNK, ROUTE_CHUNK)
        b = lax.bitcast_convert_type(aff_ref[:, pl.ds(off, ROUTE_CHUNK)], jnp.int32)
        gt = jnp.where(b > thr, 1.0, 0.0)
        eq = jnp.where(b == thr, 1.0, 0.0)
        pre = _dot(jnp.concatenate([gt, eq], axis=0).astype(MXU_DTYPE), tri)
        pg = pre[:N_EXPERTS] + cg
        pe = pre[N_EXPERTS:] + ce
        before = (pg - gt) + jnp.minimum(pe - eq, need)
        chosen = (gt > 0.0) | ((eq > 0.0) & ((pe - eq) < need))
        before_i = before.astype(jnp.int32)
        cb_ref[:, pl.ds(off, ROUTE_CHUNK)] = before_i
        sel_ref[:, pl.ds(off, ROUTE_CHUNK)] = jnp.where(chosen, before_i, -1)
        return pg[:, ROUTE_CHUNK - 1:ROUTE_CHUNK], pe[:, ROUTE_CHUNK - 1:ROUTE_CHUNK]

    zero = jnp.zeros((N_EXPERTS, 1), jnp.float32)
    lax.fori_loop(0, n // ROUTE_CHUNK, chunk, (zero, zero))


def _route(aff, cap):
    n = aff.shape[1]
    full = lambda i: (0, 0)
    return pl.pallas_call(
        functools.partial(_route_kernel, cap=cap, n=n),
        grid=(1,),
        in_specs=[pl.BlockSpec((N_EXPERTS, n), full)],
        out_specs=[pl.BlockSpec((N_EXPERTS, n), full), pl.BlockSpec((N_EXPERTS, n), full)],
        out_shape=[jax.ShapeDtypeStruct((N_EXPERTS, n), jnp.int32)] * 2,
        compiler_params=pltpu.CompilerParams(dimension_semantics=("arbitrary",), vmem_limit_bytes=VMEM_LIMIT),
        name="route_select",
    )(aff)


def _block_table(ptab_ref, i, tab_e, tab_base, tab_lo, slot, limit):
    def per_expert(e, nb):
        p0 = ptab_ref[e, i]
        nblk = jnp.right_shift(ptab_ref[e, i + 1] - p0 + (CHUNK - 1), 5)

        def per_block(kk, b):
            tab_e[slot, b] = e
            tab_lo[slot, b] = p0 + kk * CHUNK
            tab_base[slot, b] = jnp.minimum(p0 + kk * CHUNK, limit)
            return b + 1

        return lax.fori_loop(0, nblk, per_block, nb)

    nb = jnp.int32(0)
    for e in range(N_EXPERTS):
        nb = per_expert(e, nb)
    return nb


def _dispatch_kernel(ptab_ref, sel_ref, h_ref, xs_ref, oh_ref, res_ref, tab_e, tab_base, tab_lo, cnt_ref, sem, *,
                     cap, tm):
    i = pl.program_id(0)
    last = pl.num_programs(0) - 1

    def chunk_copy(b):
        e = tab_e[0, b]
        base = tab_base[0, b]
        per = tm // CHUNK
        src = res_ref.at[b // per, pl.ds(pl.multiple_of(lax.rem(b, per) * CHUNK, CHUNK), CHUNK)]
        return pltpu.make_async_copy(src, xs_ref.at[e, pl.ds(base, CHUNK)], sem.at[0])

    def wait_all(nb):
        def w(b, c):
            chunk_copy(b).wait()
            return c
        lax.fori_loop(0, nb, w, 0)

    @pl.when(i == 0)
    def _():
        oh_ref[...] = jnp.zeros(oh_ref.shape, oh_ref.dtype)
        cnt_ref[0] = 0
        res_ref[0] = jnp.zeros(res_ref.shape[1:], res_ref.dtype)
        pads = [pltpu.make_async_copy(res_ref.at[0, pl.ds(0, CHUNK)], xs_ref.at[e, pl.ds(cap, CHUNK)], sem.at[0])
                for e in range(N_EXPERTS)]
        for c in pads:
            c.start()
        for c in pads:
            c.wait()

    wait_all(cnt_ref[0])
    nb = _block_table(ptab_ref, i, tab_e, tab_base, tab_lo, 0, cap)

    def build(b, c):
        e = tab_e[0, b]
        slots = tab_base[0, b] + lax.broadcasted_iota(jnp.int32, (CHUNK, tm), 0)
        hit = sel_ref[pl.ds(e, 1), :] == slots
        oh_ref[pl.ds(pl.multiple_of(b * CHUNK, CHUNK), CHUNK), :] = jnp.where(hit, 1.0, 0.0).astype(oh_ref.dtype)
        return c

    lax.fori_loop(0, nb, build, 0)

    def group(gi, c):
        rows = pl.ds(pl.multiple_of(gi * tm, tm), tm)
        res_ref[gi] = _dot(oh_ref[rows, :], h_ref[...]).astype(res_ref.dtype).reshape((tm,) + ROW_TILE)
        return c

    lax.fori_loop(0, jnp.right_shift(nb * CHUNK + tm - 1, tm.bit_length() - 1), group, 0)

    def issue(b, c):
        chunk_copy(b).start()
        return c

    lax.fori_loop(0, nb, issue, 0)
    cnt_ref[0] = nb

    @pl.when(i == last)
    def _():
        wait_all(nb)
        cnt_ref[0] = 0


def _dispatch(ptab, sel, h2, cap, tm):
    n = h2.shape[0]
    grid_spec = pltpu.PrefetchScalarGridSpec(
        num_scalar_prefetch=1,
        grid=(n // tm,),
        in_specs=[pl.BlockSpec((N_EXPERTS, tm), lambda i, p: (0, i)),
                  pl.BlockSpec((tm, D_MODEL), lambda i, p: (i, 0))],
        out_specs=pl.BlockSpec(memory_space=pl.ANY),
        scratch_shapes=[pltpu.VMEM((MAX_BLOCKS * CHUNK, tm), MXU_DTYPE),
                        pltpu.VMEM((MAX_BLOCKS * CHUNK // tm, tm) + ROW_TILE, MXU_DTYPE),
                        pltpu.SMEM((1, MAX_BLOCKS), jnp.int32), pltpu.SMEM((1, MAX_BLOCKS), jnp.int32),
                        pltpu.SMEM((1, MAX_BLOCKS), jnp.int32),
                        pltpu.SMEM((1,), jnp.int32), pltpu.SemaphoreType.DMA((1,))],
    )
    return pl.pallas_call(
        functools.partial(_dispatch_kernel, cap=cap, tm=tm),
        grid_spec=grid_spec,
        out_shape=jax.ShapeDtypeStruct((N_EXPERTS, cap + CHUNK) + ROW_TILE, MXU_DTYPE),
        compiler_params=pltpu.CompilerParams(dimension_semantics=("arbitrary",), vmem_limit_bytes=VMEM_LIMIT),
        name="moe_dispatch",
    )(ptab, sel, h2)


def _combine_kernel(ptab_ref, sel_ref, aff_ref, y_ref, ye_ref, o_ref, stage_ref, wt_ref, tab_e, tab_base, tab_lo,
                    cnt_ref, sem, *, cap, tm):
    i = pl.program_id(0)
    nt = pl.num_programs(0)
    slot = lax.rem(i, 2)

    def chunk_copy(s, b):
        e = tab_e[s, b]
        base = tab_base[s, b]
        dst = stage_ref.at[s, pl.ds(pl.multiple_of(b * CHUNK, CHUNK), CHUNK)]
        return pltpu.make_async_copy(ye_ref.at[e, pl.ds(base, CHUNK)], dst, sem.at[s])

    def fetch(tile, s):
        nb = _block_table(ptab_ref, tile, tab_e, tab_base, tab_lo, s, cap - CHUNK)
        cnt_ref[s] = nb

        def issue(b, c):
            chunk_copy(s, b).start()
            return c

        lax.fori_loop(0, nb, issue, 0)

    @pl.when(i == 0)
    def _():
        stage_ref[...] = jnp.zeros(stage_ref.shape, stage_ref.dtype)
        fetch(i, slot)

    @pl.when(i + 1 < nt)
    def _():
        fetch(i + 1, 1 - slot)

    nb = cnt_ref[slot]

    def wait(b, c):
        chunk_copy(slot, b).wait()
        return c

    lax.fori_loop(0, nb, wait, 0)
    ngroups = jnp.right_shift(nb * CHUNK + tm - 1, tm.bit_length() - 1)

    def build(b, c):
        e = tab_e[slot, b]
        slots = tab_base[slot, b] + lax.broadcasted_iota(jnp.int32, (CHUNK, tm), 0)
        hit = (sel_ref[pl.ds(e, 1), :] == slots) & (slots >= tab_lo[slot, b])
        w = jnp.where(hit, aff_ref[pl.ds(e, 1), :], 0.0)
        wt_ref[pl.ds(pl.multiple_of(b * CHUNK, CHUNK), CHUNK), :] = w.astype(wt_ref.dtype)
        return c

    def clear(b, c):
        wt_ref[pl.ds(pl.multiple_of(b * CHUNK, CHUNK), CHUNK), :] = jnp.zeros((CHUNK, tm), wt_ref.dtype)
        return c

    lax.fori_loop(0, nb, build, 0)
    lax.fori_loop(nb, ngroups * (tm // CHUNK), clear, 0)
    o_ref[...] = y_ref[...]

    def group(gi, c):
        rows = pl.ds(pl.multiple_of(gi * tm, tm), tm)
        o_ref[...] += lax.dot_general(wt_ref[rows, :], stage_ref[slot, rows].reshape(tm, D_MODEL),
                                      (((0,), (0,)), ((), ())), preferred_element_type=jnp.float32)
        return c

    lax.fori_loop(0, ngroups, group, 0)


def _combine(ptab, sel, aff, y, ye, cap, tm):
    n = y.shape[0]
    grid_spec = pltpu.PrefetchScalarGridSpec(
        num_scalar_prefetch=1,
        grid=(n // tm,),
        in_specs=[pl.BlockSpec((N_EXPERTS, tm), lambda i, p: (0, i)),
                  pl.BlockSpec((N_EXPERTS, tm), lambda i, p: (0, i)),
                  pl.BlockSpec((tm, D_MODEL), lambda i, p: (i, 0)),
                  pl.BlockSpec(memory_space=pl.ANY)],
        out_specs=pl.BlockSpec((tm, D_MODEL), lambda i, p: (i, 0)),
        scratch_shapes=[pltpu.VMEM((2, MAX_BLOCKS * CHUNK) + ROW_TILE, MXU_DTYPE),
                        pltpu.VMEM((MAX_BLOCKS * CHUNK, tm), MXU_DTYPE),
                        pltpu.SMEM((2, MAX_BLOCKS), jnp.int32), pltpu.SMEM((2, MAX_BLOCKS), jnp.int32),
                        pltpu.SMEM((2, MAX_BLOCKS), jnp.int32),
                        pltpu.SMEM((2,), jnp.int32), pltpu.SemaphoreType.DMA((2,))],
    )
    return pl.pallas_call(
        functools.partial(_combine_kernel, cap=cap, tm=tm),
        grid_spec=grid_spec,
        out_shape=jax.ShapeDtypeStruct((n, D_MODEL), jnp.float32),
        compiler_params=pltpu.CompilerParams(dimension_semantics=("arbitrary",), vmem_limit_bytes=VMEM_LIMIT),
        name="moe_combine",
    )(ptab, sel, aff, y, ye)


def _rope_tables(seq):
    pos = jnp.arange(seq, dtype=jnp.int32)
    inv = jnp.power(ROPE_THETA, -jnp.arange(0, 32, 2, dtype=jnp.float32) / 32.0)
    ang_row = (pos // GRID_W).astype(jnp.float32)[:, None] * inv[None, :]
    ang_col = (pos % GRID_W).astype(jnp.float32)[:, None] * inv[None, :]
    ang_seq = pos.astype(jnp.float32)[:, None] * inv[None, :]
    sign = jnp.concatenate([-jnp.ones((16,), jnp.float32), jnp.ones((16,), jnp.float32)])

    def pair(ang):
        return jnp.tile(jnp.cos(ang), (1, 2)), jnp.tile(jnp.sin(ang), (1, 2)) * sign[None, :]

    cr, sr = pair(ang_row)
    cc, sc = pair(ang_col)
    cs, ss = pair(ang_seq)
    one = jnp.ones((seq, 32), jnp.float32)
    zero = jnp.zeros((seq, 32), jnp.float32)
    return {
        "cosa": jnp.concatenate([cr, cc, cr, cc], axis=1),
        "sina": jnp.concatenate([sr, sc, sr, sc], axis=1),
        "cosb": jnp.concatenate([one, one, cs, one], axis=1),
        "sinb": jnp.concatenate([zero, zero, ss, zero], axis=1),
    }


def _prep_layer(l, p):
    f32 = jnp.float32
    w_in = p["w_in"][l]
    a0, b0, c0, x0, g0 = 0, 768, 1440, 2208, 2464

    kr = jnp.zeros((D_MODEL, LANES), f32).at[:, 64:96].set(w_in[:, b0 + 640:b0 + 672])
    w1 = jnp.concatenate([
        w_in[:, a0:a0 + 768],
        w_in[:, b0:b0 + 384], w_in[:, b0 + 384:b0 + 640],
        w_in[:, c0:c0 + 768],
        w_in[:, x0:x0 + 256], kr], axis=1)
    assert w1.shape[1] == _C_END

    wqb = jnp.zeros((B_Q_RANK, B_HEADS, LANES), f32).at[:, :, :B_QK].set(
        p["b_w_q_b"][l].reshape(B_Q_RANK, B_HEADS, B_QK)).reshape(B_Q_RANK, B_HEADS * LANES)
    wkv = p["b_w_kv_b"][l].reshape(B_KV_RANK, B_HEADS, 2 * B_NOPE)
    wkb = jnp.zeros((B_KV_RANK, B_HEADS, LANES), f32).at[:, :, :B_NOPE].set(
        wkv[:, :, :B_NOPE]).reshape(B_KV_RANK, B_HEADS * LANES)
    wvb = wkv[:, :, B_NOPE:].reshape(B_KV_RANK, B_HEADS * B_NOPE)

    def g64(g, scale=1.0):
        return jnp.tile(g * scale, 2)

    def g96(g, scale=1.0):
        return jnp.concatenate([g * scale, jnp.zeros((LANES - B_QK,), f32)])

    gains = jnp.stack([
        g64(p["a_q_norm"][l], HEAD ** -0.5 * LOG2E), g64(p["a_k_norm"][l]),
        g96(p["b_q_norm"][l], B_QK ** -0.5 * LOG2E), g96(p["b_k_norm"][l]),
        g64(p["c_q_norm"][l], HEAD ** -0.5 * LOG2E), g64(p["c_k_norm"][l]),
        g64(p["x_q_norm"][l], HEAD ** -0.5 * LOG2E), jnp.zeros((LANES,), f32)])

    wr = p["w_router"][l].T
    wrh = wr.astype(MXU_DTYPE)
    wrl = (wr - wrh.astype(f32)).astype(MXU_DTYPE)
    bf = lambda a: a.astype(MXU_DTYPE)
    return {
        "g1": p["norm1_g"][l][None, :], "w1": bf(w1), "wqb": bf(wqb), "wkb": bf(wkb), "wvb": bf(wvb),
        "gains": gains, "gbq": p["b_q_a_norm"][l][None, :], "gbkv": p["b_kv_a_norm"][l][None, :],
        "gmem": p["mem_norm_g"][l][None, :], "wmem": bf(p["w_mem_kv"][l]), "gxk": g64(p["x_k_norm"][l])[None, :],
        "sink": p["c_sink"][l],
        "wg": bf(w_in[:, g0:]), "wbr": bf(p["w_branch"][l]), "wo": bf(p["w_out"][l]),
        "g2": p["norm2_g"][l][None, :], "wrh": wrh, "wrl": wrl,
        "wgu": bf(p["w_gate_up"][l]), "wd": bf(p["w_down"][l]),
    }


def _trunk(x, mem, layers):
    batch, seq, _ = x.shape
    n = batch * seq
    tm = min(256, seq)
    tp = min(512, seq)
    cap = EC_FACTOR * n // N_EXPERTS
    tc = min(512, cap)
    tabs = _rope_tables(seq)
    x2 = x.reshape(n, D_MODEL)
    mem2 = mem.reshape(batch * N_MEM, D_MODEL)
    for lw in layers:
        qa, ka, va, qb, kb, vb, qc, kc, vc, qx = _in_projection(x2, lw, tabs, seq, tp)
        kx, vx = _memory_kv(mem2, lw)
        oa = _flash_attention(qa, ka, va, mode="gqa", batch=batch, seq=seq, kv_len=seq, tq=TQ_A, tk=TK_A)
        ob = _flash_attention(qb, kb, vb, mode="mha", batch=batch, seq=seq, kv_len=seq, tq=TQ_B, tk=TK_B)
        oc = _window_attention(qc, kc, vc, lw["sink"], batch=batch, seq=seq, tq=min(256, seq - 2 * C_WINDOW))
        ox = _flash_attention(qx, kx, vx, mode="cross", batch=batch, seq=seq, kv_len=N_MEM, tq=TQ_X, tk=N_MEM)
        y, h2, aff = _merge(x2, oa, ob, oc, ox, lw, tp)
        sel, before = _route(aff, cap)
        ptab = jnp.concatenate([before[:, ::tm], jnp.full((N_EXPERTS, 1), cap, jnp.int32)], axis=1)
        xs = _dispatch(ptab, sel, h2, cap, tm)
        ye = _expert_ffn(xs, lw["wgu"], lw["wd"], cap, tc)
        x2 = _combine(ptab, sel, aff, y, ye, cap, tm)
    return x2.reshape(batch, seq, D_MODEL)


def kernel(x_prompt, x_sample, mem_prompt, mem_sample, norm1_g, w_in, a_q_norm, a_k_norm, b_q_a_norm,
           b_w_q_b, b_kv_a_norm, b_w_kv_b, b_q_norm, b_k_norm, c_q_norm, c_k_norm, c_sink, x_q_norm,
           x_k_norm, mem_norm_g, w_mem_kv, w_branch, w_out, norm2_g, w_router, w_gate_up, w_down):
    p = dict(norm1_g=norm1_g, w_in=w_in, a_q_norm=a_q_norm, a_k_norm=a_k_norm, b_q_a_norm=b_q_a_norm,
             b_w_q_b=b_w_q_b, b_kv_a_norm=b_kv_a_norm, b_w_kv_b=b_w_kv_b, b_q_norm=b_q_norm, b_k_norm=b_k_norm,
             c_q_norm=c_q_norm, c_k_norm=c_k_norm, c_sink=c_sink, x_q_norm=x_q_norm, x_k_norm=x_k_norm,
             mem_norm_g=mem_norm_g, w_mem_kv=w_mem_kv, w_branch=w_branch, w_out=w_out, norm2_g=norm2_g,
             w_router=w_router, w_gate_up=w_gate_up, w_down=w_down)
    layers = [_prep_layer(l, p) for l in range(w_in.shape[0])]
    return (_trunk(x_prompt, mem_prompt, layers), _trunk(x_sample, mem_sample, layers))
```

```python
import functools

import jax
import jax.numpy as jnp
from jax import lax
from jax.experimental import pallas as pl
from jax.experimental.pallas import tpu as pltpu

D_MODEL = 1024
GRID_W = 64
N_MEM = 256
EPS = 1e-6
ROPE_THETA = 10000.0
NEG_INF = -1e30
HEAD = 64
A_KV_HEADS = 2
B_HEADS = 8
B_Q_RANK = 384
B_KV_RANK = 256
B_NOPE = 64
B_ROPE = 32
B_QK = B_NOPE + B_ROPE
C_WINDOW = 128
X_HEADS = 4
N_EXPERTS = 16
EC_FACTOR = 2
D_EXPERT = 1024
LANES = 128

MXU_DTYPE = jnp.bfloat16
VMEM_LIMIT = 56 * 1024 * 1024
LOG2E = 1.4426950408889634
TQ_A, TK_A = 1024, 512
TQ_B, TK_B = 2048, 512
TQ_X = 1024
FLASH_COLS = 256

_C_AQ, _C_AK, _C_AV = 0, 512, 640
_C_BCQ, _C_BCKV = 768, 1152
_C_CQ, _C_CK, _C_CV = 1408, 1920, 2048
_C_XQ, _C_KR, _C_END = 2176, 2432, 2560
V_ROWS = 80


def _dot(a, b):
    return jnp.dot(a, b, preferred_element_type=jnp.float32)


def _dot_nt(a, b):
    return lax.dot_general(a, b, (((1,), (1,)), ((), ())), preferred_element_type=jnp.float32)


def _rms(x, g):
    return x * lax.rsqrt(jnp.mean(x * x, axis=-1, keepdims=True) + EPS) * g


def _block_ones(head_w):
    w = 2 * LANES
    row = lax.broadcasted_iota(jnp.int32, (w, w), 0) // head_w
    col = lax.broadcasted_iota(jnp.int32, (w, w), 1) // head_w
    return jnp.where(row == col, 1.0, 0.0).astype(MXU_DTYPE)


def _head_norm(z, gain, ones, n_valid):
    width = z.shape[1]
    outs = []
    c = 0
    while c < width:
        w = min(2 * LANES, width - c)
        blk = z[:, c:c + w]
        ss = _dot((blk * blk).astype(MXU_DTYPE), ones[:w, :w])
        g = gain if w == LANES else jnp.concatenate([gain, gain], axis=1)
        normed = blk * lax.rsqrt(ss * (1.0 / n_valid) + EPS) * g
        outs += [normed[:, i * LANES:(i + 1) * LANES] for i in range(w // LANES)]
        c += w
    return outs


def _rope(blocks, cos, sin_signed):
    lane = lax.broadcasted_iota(jnp.int32, (1, LANES), 1)
    first = (lane % 32) < 16
    outs = []
    for blk in blocks:
        up = pltpu.roll(blk, LANES - 16, 1)
        dn = pltpu.roll(blk, 16, 1)
        outs.append(blk * cos + jnp.where(first, up, dn) * sin_signed)
    return outs


def _cat(blocks, dtype):
    return jnp.concatenate(blocks, axis=1).astype(dtype)


def _transposed_v(v, dtype):
    vt = v.T
    ones = jnp.ones((V_ROWS - HEAD, v.shape[0]), jnp.float32)
    parts = []
    for h in range(v.shape[1] // HEAD):
        parts += [vt[h * HEAD:(h + 1) * HEAD, :], ones]
    return jnp.concatenate(parts, axis=0).astype(dtype)


def _inproj_kernel(x_ref, g1_ref, w1_ref, wqb_ref, wkb_ref, wvb_ref, gains_ref, gbq_ref, gbkv_ref,
                   cosa_ref, sina_ref, cosb_ref, sinb_ref,
                   qa_ref, ka_ref, va_ref, qb_ref, kb_ref, vb_ref, qc_ref, kc_ref, vc_ref, qx_ref):
    x = x_ref[...]
    hb = _rms(x, g1_ref[...]).astype(MXU_DTYPE)
    z = _dot(hb, w1_ref[...])
    gains = gains_ref[...]
    cosa, sina = cosa_ref[...], sina_ref[...]
    cosb, sinb = cosb_ref[...], sinb_ref[...]
    od = qa_ref.dtype
    ones64, ones128 = _block_ones(HEAD), _block_ones(LANES)

    qa = _rope(_head_norm(z[:, _C_AQ:_C_AK], gains[0:1], ones64, HEAD), cosa, sina)
    qa_ref[...] = _cat(qa, od)
    ka = _rope(_head_norm(z[:, _C_AK:_C_AV], gains[1:2], ones64, HEAD), cosa, sina)
    ka_ref[...] = ka[0].astype(od)
    va_ref[...] = _transposed_v(z[:, _C_AV:_C_BCQ], od)

    cq = _rms(z[:, _C_BCQ:_C_BCKV], gbq_ref[...]).astype(MXU_DTYPE)
    ckv = _rms(z[:, _C_BCKV:_C_CQ], gbkv_ref[...]).astype(MXU_DTYPE)
    q8 = _dot(cq, wqb_ref[...])
    k8 = _dot(ckv, wkb_ref[...])
    kr = z[:, _C_KR:_C_END]
    k8 = k8 + jnp.concatenate([kr] * B_HEADS, axis=1)
    qb_ref[...] = _cat(_rope(_head_norm(q8, gains[2:3], ones128, B_QK), cosb, sinb), od)
    kb_ref[...] = _cat(_rope(_head_norm(k8, gains[3:4], ones128, B_QK), cosb, sinb), od)
    vb_ref[...] = _transposed_v(_dot(ckv, wvb_ref[...]), od)

    qc_ref[...] = _cat(_head_norm(z[:, _C_CQ:_C_CK], gains[4:5], ones64, HEAD), od)
    kc_ref[...] = _head_norm(z[:, _C_CK:_C_CV], gains[5:6], ones64, HEAD)[0].astype(od)
    vc_ref[...] = _transposed_v(z[:, _C_CV:_C_XQ], od)

    qx_ref[...] = _cat(_head_norm(z[:, _C_XQ:_C_KR], gains[6:7], ones64, HEAD), od)


def _in_projection(x2, lw, tabs, seq, tm):
    n = x2.shape[0]
    nt = seq // tm
    row = lambda i: (i, 0)
    full = lambda i: (0, 0)
    tab = lambda i: (i % nt, 0)
    col = lambda i: (0, i)
    outs_def = [(512, False), (LANES, False), (A_KV_HEADS * V_ROWS, True), (1024, False), (1024, False),
                (B_HEADS * V_ROWS, True), (512, False), (LANES, False), (A_KV_HEADS * V_ROWS, True), (256, False)]
    out_specs = [pl.BlockSpec((w, tm), col) if t else pl.BlockSpec((tm, w), row) for w, t in outs_def]
    out_shape = [jax.ShapeDtypeStruct((w, n) if t else (n, w), MXU_DTYPE) for w, t in outs_def]
    outs = pl.pallas_call(
        _inproj_kernel,
        grid=(n // tm,),
        in_specs=[
            pl.BlockSpec((tm, D_MODEL), row),
            pl.BlockSpec((1, D_MODEL), full),
            pl.BlockSpec(lw["w1"].shape, full),
            pl.BlockSpec(lw["wqb"].shape, full),
            pl.BlockSpec(lw["wkb"].shape, full),
            pl.BlockSpec(lw["wvb"].shape, full),
            pl.BlockSpec(lw["gains"].shape, full),
            pl.BlockSpec((1, B_Q_RANK), full),
            pl.BlockSpec((1, B_KV_RANK), full),
            pl.BlockSpec((tm, LANES), tab),
            pl.BlockSpec((tm, LANES), tab),
            pl.BlockSpec((tm, LANES), tab),
            pl.BlockSpec((tm, LANES), tab),
        ],
        out_specs=out_specs,
        out_shape=out_shape,
        compiler_params=pltpu.CompilerParams(dimension_semantics=("parallel",), vmem_limit_bytes=VMEM_LIMIT),
        name="in_projection",
    )(x2, lw["g1"], lw["w1"], lw["wqb"], lw["wkb"], lw["wvb"], lw["gains"], lw["gbq"], lw["gbkv"],
      tabs["cosa"], tabs["sina"], tabs["cosb"], tabs["sinb"])
    return outs


def _memkv_kernel(mem_ref, g_ref, w_ref, gain_ref, k_ref, v_ref):
    m = _rms(mem_ref[...], g_ref[...]).astype(MXU_DTYPE)
    kv = _dot(m, w_ref[...])
    half = kv.shape[1] // 2
    k_ref[...] = _cat(_head_norm(kv[:, :half], gain_ref[...], _block_ones(HEAD), HEAD), k_ref.dtype)
    v_ref[...] = _transposed_v(kv[:, half:], v_ref.dtype)


def _memory_kv(mem2, lw):
    n = mem2.shape[0]
    tm = N_MEM
    xq = X_HEADS * HEAD
    row = lambda i: (i, 0)
    full = lambda i: (0, 0)
    return pl.pallas_call(
        _memkv_kernel,
        grid=(n // tm,),
        in_specs=[pl.BlockSpec((tm, D_MODEL), row), pl.BlockSpec((1, D_MODEL), full),
                  pl.BlockSpec((D_MODEL, 2 * xq), full), pl.BlockSpec((1, LANES), full)],
        out_specs=[pl.BlockSpec((tm, xq), row), pl.BlockSpec((X_HEADS * V_ROWS, tm), lambda i: (0, i))],
        out_shape=[jax.ShapeDtypeStruct((n, xq), MXU_DTYPE),
                   jax.ShapeDtypeStruct((X_HEADS * V_ROWS, n), MXU_DTYPE)],
        compiler_params=pltpu.CompilerParams(dimension_semantics=("parallel",)),
        name="memory_kv",
    )(mem2, lw["gmem"], lw["wmem"], lw["gxk"])


def _flash_kernel(q_ref, k_ref, v_ref, o_ref, qs_ref, s_ref, m_ref, acc_ref, *, mode, nh, tq, tk, n_kv):
    g = pl.program_id(1)
    qt = q_ref[...].astype(jnp.float32).T
    kdim = qs_ref.shape[0]
    for i in range(nh):
        if mode == "gqa":
            blk = qt[i * HEAD:(i + 1) * HEAD, :]
            half = lax.broadcasted_iota(jnp.int32, (kdim, tq), 0) // HEAD
            op = jnp.where(half == g, jnp.concatenate([blk, blk], axis=0), 0.0)
        elif mode == "mha":
            op = qt[i * LANES:(i + 1) * LANES, :]
        else:
            head = lax.broadcasted_iota(jnp.int32, (kdim, tq), 0) // HEAD
            op = jnp.where(head == i, qt, 0.0)
        qs_ref[:, i * tq:(i + 1) * tq] = op.astype(qs_ref.dtype)
    m_ref[...] = jnp.full(m_ref.shape, -jnp.inf, jnp.float32)
    acc_ref[...] = jnp.zeros(acc_ref.shape, jnp.float32)

    ncols = nh * tq
    cw = min(tq if mode == "mha" else ncols, FLASH_COLS)
    chunks = [(c, c + cw) for c in range(0, ncols, cw)]

    def score(j, slot, c0, c1):
        off = pl.multiple_of(j * tk, tk)
        if mode == "mha":
            i = c0 // tq
            s_ref[slot, :, c0:c1] = _dot(k_ref[pl.ds(off, tk), i * LANES:(i + 1) * LANES], qs_ref[:, c0:c1])
        else:
            s_ref[slot, :, c0:c1] = _dot(k_ref[pl.ds(off, tk), :], qs_ref[:, c0:c1])

    def consume(j, slot, c0, c1):
        off = pl.multiple_of(j * tk, tk)
        s = s_ref[slot, :, c0:c1]
        m_prev = m_ref[0:1, c0:c1]
        m_new = jnp.maximum(m_prev, jnp.max(s, axis=0, keepdims=True))
        alpha = jnp.exp2(m_prev - m_new)
        p = jnp.exp2(s - m_new).astype(v_ref.dtype)
        if mode == "gqa":
            pv = _dot(v_ref[:, pl.ds(off, tk)], p)
        else:
            pv = jnp.concatenate(
                [_dot(v_ref[i * V_ROWS:(i + 1) * V_ROWS, pl.ds(off, tk)],
                      p[:, max(i * tq, c0) - c0:min((i + 1) * tq, c1) - c0])
                 for i in range(c0 // tq, (c1 + tq - 1) // tq)], axis=1)
        acc_ref[:, c0:c1] = alpha * acc_ref[:, c0:c1] + pv
        m_ref[0:1, c0:c1] = m_new

    def step(js, slot_s, jc, slot_c):
        for c0, c1 in chunks:
            if js is not None:
                score(js, slot_s, c0, c1)
            if jc is not None:
                consume(jc, slot_c, c0, c1)

    step(0, 0, None, None)
    if n_kv > 1:
        def pair(jj, carry):
            step(2 * jj + 1, 1, 2 * jj, 0)
            step(2 * jj + 2, 0, 2 * jj + 1, 1)
            return carry

        lax.fori_loop(0, n_kv // 2 - 1, pair, 0)
        step(n_kv - 1, 1, n_kv - 2, 0)
        step(None, None, n_kv - 1, 1)
    else:
        step(None, None, 0, 0)
    a = acc_ref[...]
    ot = a[:HEAD, :] / a[HEAD:HEAD + 1, :]
    o_ref[...] = jnp.concatenate([ot[:, i * tq:(i + 1) * tq] for i in range(nh)], axis=0).T.astype(o_ref.dtype)


def _flash_attention(q, k, vt, *, mode, batch, seq, kv_len, tq, tk):
    kw = 256
    groups = q.shape[1] // kw
    nh = {"gqa": 4, "mha": 2, "cross": 4}[mode]
    kcols = {"gqa": LANES, "mha": kw, "cross": kw}[mode]
    kdim = {"gqa": LANES, "mha": LANES, "cross": kw}[mode]
    vrows = V_ROWS if mode == "gqa" else nh * V_ROWS
    k_map = (lambda b, g, i: (b, 0)) if mode == "gqa" else (lambda b, g, i: (b, g))
    tq = min(tq, seq)
    tk = min(tk, kv_len)
    nq = seq // tq
    ow = nh * HEAD
    kern = functools.partial(_flash_kernel, mode=mode, nh=nh, tq=tq, tk=tk, n_kv=kv_len // tk)
    return pl.pallas_call(
        kern,
        grid=(batch, groups, nq),
        in_specs=[pl.BlockSpec((tq, kw), lambda b, g, i: (b * nq + i, g)),
                  pl.BlockSpec((kv_len, kcols), k_map),
                  pl.BlockSpec((vrows, kv_len), lambda b, g, i: (g, b))],
        out_specs=pl.BlockSpec((tq, ow), lambda b, g, i: (b * nq + i, g)),
        out_shape=jax.ShapeDtypeStruct((batch * seq, groups * ow), MXU_DTYPE),
        scratch_shapes=[pltpu.VMEM((kdim, nh * tq), MXU_DTYPE), pltpu.VMEM((2, tk, nh * tq), jnp.float32),
                        pltpu.VMEM((8, nh * tq), jnp.float32), pltpu.VMEM((V_ROWS, nh * tq), jnp.float32)],
        compiler_params=pltpu.CompilerParams(dimension_semantics=("parallel", "parallel", "parallel"),
                                             vmem_limit_bytes=VMEM_LIMIT),
        name=f"flash_attention_{mode}",
    )(q, k, vt)


def _window_kernel(sink_ref, q_ref, k_ref, v_ref, o_ref, *, tq, seq):
    g = pl.program_id(1)
    j = pl.program_id(2)
    nh = 4
    win = tq + 2 * C_WINDOW
    start = pl.multiple_of(jnp.clip(j * tq - C_WINDOW, 0, seq - win), C_WINDOW)
    k = k_ref[pl.ds(start, win), :]
    vt = v_ref[:, pl.ds(start, win)]
    qt = q_ref[...].astype(jnp.float32).T
    half = lax.broadcasted_iota(jnp.int32, (2 * HEAD, tq), 0) // HEAD
    ops = []
    for i in range(nh):
        blk = qt[i * HEAD:(i + 1) * HEAD, :]
        ops.append(jnp.where(half == g, jnp.concatenate([blk, blk], axis=0), 0.0))
    s = _dot(k, jnp.concatenate(ops, axis=1).astype(k.dtype))
    spos = start + lax.broadcasted_iota(jnp.int32, (win, tq), 0)
    tpos = j * tq + lax.broadcasted_iota(jnp.int32, (win, tq), 1)
    rel = jnp.abs(spos - tpos)
    dist = jnp.where(rel <= C_WINDOW, rel.astype(jnp.float32), -NEG_INF)
    ps, sinks = [], []
    for i in range(nh):
        slope = jnp.where(g == 0, 2.0 ** -(i + 1), 2.0 ** -(i + 5)) * LOG2E
        si = s[:, i * tq:(i + 1) * tq] - slope * dist
        sink = sink_ref[nh * g + i] * LOG2E
        m = jnp.maximum(jnp.max(si, axis=0, keepdims=True), sink)
        ps.append(jnp.exp2(si - m).astype(vt.dtype))
        sinks.append(jnp.exp2(sink - m))
    acc = _dot(vt, jnp.concatenate(ps, axis=1))
    ot = acc[:HEAD, :] / (acc[HEAD:HEAD + 1, :] + jnp.concatenate(sinks, axis=1))
    o_ref[...] = jnp.concatenate([ot[:, i * tq:(i + 1) * tq] for i in range(nh)], axis=0).T.astype(o_ref.dtype)


def _window_attention(q, k, v, sink, *, batch, seq, tq):
    kw = 256
    groups = q.shape[1] // kw
    nq = seq // tq
    kern = functools.partial(_window_kernel, tq=tq, seq=seq)
    return pl.pallas_call(
        kern,
        grid=(batch, groups, nq),
        in_specs=[pl.BlockSpec(memory_space=pltpu.SMEM),
                  pl.BlockSpec((tq, kw), lambda b, g, i: (b * nq + i, g)),
                  pl.BlockSpec((seq, LANES), lambda b, g, i: (b, 0)),
                  pl.BlockSpec((V_ROWS, seq), lambda b, g, i: (g, b))],
        out_specs=pl.BlockSpec((tq, kw), lambda b, g, i: (b * nq + i, g)),
        out_shape=jax.ShapeDtypeStruct((batch * seq, groups * kw), MXU_DTYPE),
        compiler_params=pltpu.CompilerParams(dimension_semantics=("parallel", "parallel", "parallel"),
                                             vmem_limit_bytes=VMEM_LIMIT),
        name="window_attention",
    )(sink, q, k, v)


def _merge_kernel(x_ref, oa_ref, ob_ref, oc_ref, ox_ref, g1_ref, wg_ref, wbr_ref, wo_ref, g2_ref,
                  wrh_ref, wrl_ref, y_ref, h2_ref, aff_ref):
    x = x_ref[...]
    hb = _rms(x, g1_ref[...]).astype(MXU_DTYPE)
    merged = jnp.zeros(x.shape, jnp.float32)
    r0 = 0
    for i, o_ref in enumerate((oa_ref, ob_ref, oc_ref, ox_ref)):
        r1 = r0 + o_ref.shape[1]
        gate = jax.nn.sigmoid(_dot(hb, wg_ref[:, i * D_MODEL:(i + 1) * D_MODEL]))
        merged = merged + gate * _dot(o_ref[...], wbr_ref[r0:r1, :])
        r0 = r1
    y = x + _dot(merged.astype(MXU_DTYPE), wo_ref[...])
    y_ref[...] = y
    h2 = _rms(y, g2_ref[...])
    h2_hi = h2.astype(MXU_DTYPE)
    h2_lo = (h2 - h2_hi.astype(jnp.float32)).astype(MXU_DTYPE)
    h2_ref[...] = h2_hi
    wrh, wrl = wrh_ref[...], wrl_ref[...]
    logits = _dot_nt(wrh, h2_hi) + _dot_nt(wrh, h2_lo) + _dot_nt(wrl, h2_hi)
    e = jnp.exp(logits - jnp.max(logits, axis=0, keepdims=True))
    aff_ref[...] = e / jnp.sum(e, axis=0, keepdims=True)


def _merge(x2, oa, ob, oc, ox, lw, tm):
    n = x2.shape[0]
    row = lambda i: (i, 0)
    full = lambda i: (0, 0)
    return pl.pallas_call(
        _merge_kernel,
        grid=(n // tm,),
        in_specs=[pl.BlockSpec((tm, D_MODEL), row),
                  pl.BlockSpec((tm, oa.shape[1]), row), pl.BlockSpec((tm, ob.shape[1]), row),
                  pl.BlockSpec((tm, oc.shape[1]), row), pl.BlockSpec((tm, ox.shape[1]), row),
                  pl.BlockSpec((1, D_MODEL), full),
                  pl.BlockSpec(lw["wg"].shape, full, pipeline_mode=pl.Buffered(1)),
                  pl.BlockSpec(lw["wbr"].shape, full, pipeline_mode=pl.Buffered(1)),
                  pl.BlockSpec(lw["wo"].shape, full, pipeline_mode=pl.Buffered(1)), pl.BlockSpec((1, D_MODEL), full),
                  pl.BlockSpec(lw["wrh"].shape, full), pl.BlockSpec(lw["wrl"].shape, full)],
        out_specs=[pl.BlockSpec((tm, D_MODEL), row), pl.BlockSpec((tm, D_MODEL), row),
                   pl.BlockSpec((N_EXPERTS, tm), lambda i: (0, i))],
        out_shape=[jax.ShapeDtypeStruct((n, D_MODEL), jnp.float32),
                   jax.ShapeDtypeStruct((n, D_MODEL), MXU_DTYPE),
                   jax.ShapeDtypeStruct((N_EXPERTS, n), jnp.float32)],
        compiler_params=pltpu.CompilerParams(dimension_semantics=("parallel",), vmem_limit_bytes=VMEM_LIMIT),
        name="merge_router",
    )(x2, oa, ob, oc, ox, lw["g1"], lw["wg"], lw["wbr"], lw["wo"], lw["g2"], lw["wrh"], lw["wrl"])


ROW_TILE = (8, LANES)


def _expert_kernel(xs_ref, wgu_ref, wd_ref, y_ref):
    tc = xs_ref.shape[1]
    gu = _dot(xs_ref[0].reshape(tc, D_MODEL), wgu_ref[0])
    g, u = gu[:, :D_EXPERT], gu[:, D_EXPERT:]
    act = (g * jax.nn.sigmoid(g) * u).astype(MXU_DTYPE)
    y_ref[0] = _dot(act, wd_ref[0]).astype(y_ref.dtype).reshape((tc,) + ROW_TILE)


def _expert_ffn(xs, wgu, wd, cap, tc):
    nt = cap // tc
    return pl.pallas_call(
        _expert_kernel,
        grid=(N_EXPERTS, nt),
        in_specs=[pl.BlockSpec((1, tc) + ROW_TILE, lambda e, c: (e, c, 0, 0)),
                  pl.BlockSpec((1, D_MODEL, 2 * D_EXPERT), lambda e, c: (e, 0, 0)),
                  pl.BlockSpec((1, D_EXPERT, D_MODEL), lambda e, c: (e, 0, 0))],
        out_specs=pl.BlockSpec((1, tc) + ROW_TILE, lambda e, c: (e, c, 0, 0)),
        out_shape=jax.ShapeDtypeStruct((N_EXPERTS, cap) + ROW_TILE, MXU_DTYPE),
        compiler_params=pltpu.CompilerParams(dimension_semantics=("parallel", "parallel"),
                                             vmem_limit_bytes=VMEM_LIMIT),
        name="expert_ffn",
    )(xs, wgu, wd)


CHUNK = 32
MAX_BLOCKS = 128
ROUTE_CHUNK = 256


def _route_kernel(aff_ref, sel_ref, ptab_ref, *, cap, n):
    bits = lax.bitcast_convert_type(aff_ref[...], jnp.int32)

    def search(b, thr):
        cand = thr | jnp.left_shift(jnp.int32(1), 30 - b)
        cnt = jnp.sum((bits >= cand).astype(jnp.int32), axis=1, keepdims=True)
        return jnp.where(cnt >= cap, cand, thr)

    thr = lax.fori_loop(0, 31, search, jnp.zeros((N_EXPERTS, 1), jnp.int32))
    n_gt = jnp.sum((bits > thr).astype(jnp.int32), axis=1, keepdims=True)
    need = (cap - n_gt).astype(jnp.float32)
    k = lax.broadcasted_iota(jnp.int32, (ROUTE_CHUNK, ROUTE_CHUNK), 0)
    j = lax.broadcasted_iota(jnp.int32, (ROUTE_CHUNK, ROUTE_CHUNK), 1)
    tri = jnp.where(k <= j, 1.0, 0.0).astype(MXU_DTYPE)
    tile = lax.broadcasted_iota(jnp.int32, ptab_ref.shape, 1)

    def chunk(c, carry):
        cg, ce, tab = carry
        off = pl.multiple_of(c * ROUTE_CHUNK, ROUTE_CHUNK)
        tab = jnp.where(tile == c, cg + jnp.minimum(ce, need), tab)
        b = lax.bitcast_convert_type(aff_ref[:, pl.ds(off, ROUTE_CHUNK)], jnp.int32)
        gt = jnp.where(b > thr, 1.0, 0.0)
        eq = jnp.where(b == thr, 1.0, 0.0)
        pre = _dot(jnp.concatenate([gt, eq], axis=0).astype(MXU_DTYPE), tri)
        pg = pre[:N_EXPERTS] + cg
        pe = pre[N_EXPERTS:] + ce
        before = (pg - gt) + jnp.minimum(pe - eq, need)
        chosen = (gt > 0.0) | ((eq > 0.0) & ((pe - eq) < need))
        sel_ref[:, pl.ds(off, ROUTE_CHUNK)] = jnp.where(chosen, before.astype(jnp.int32), -1)
        return pg[:, ROUTE_CHUNK - 1:ROUTE_CHUNK], pe[:, ROUTE_CHUNK - 1:ROUTE_CHUNK], tab

    zero = jnp.zeros((N_EXPERTS, 1), jnp.float32)
    n_tiles = n // ROUTE_CHUNK
    _, _, tab = lax.fori_loop(0, n_tiles, chunk, (zero, zero, jnp.zeros(ptab_ref.shape, jnp.float32)))
    ptab_ref[...] = jnp.where(tile == n_tiles, cap, tab.astype(jnp.int32))


def _route(aff, cap):
    n = aff.shape[1]
    ntp = -(-(n // ROUTE_CHUNK + 1) // LANES) * LANES
    full = lambda i: (0, 0)
    return pl.pallas_call(
        functools.partial(_route_kernel, cap=cap, n=n),
        grid=(1,),
        in_specs=[pl.BlockSpec((N_EXPERTS, n), full)],
        out_specs=[pl.BlockSpec((N_EXPERTS, n), full), pl.BlockSpec((N_EXPERTS, ntp), full)],
        out_shape=[jax.ShapeDtypeStruct((N_EXPERTS, n), jnp.int32),
                   jax.ShapeDtypeStruct((N_EXPERTS, ntp), jnp.int32)],
        compiler_params=pltpu.CompilerParams(dimension_semantics=("arbitrary",), vmem_limit_bytes=VMEM_LIMIT),
        name="route_select",
    )(aff)


def _block_table(ptab_ref, i, tab_e, tab_base, tab_lo, slot, limit):
    def per_expert(e, nb):
        p0 = ptab_ref[e, i]
        nblk = jnp.right_shift(ptab_ref[e, i + 1] - p0 + (CHUNK - 1), 5)

        def per_block(kk, b):
            tab_e[slot, b] = e
            tab_lo[slot, b] = p0 + kk * CHUNK
            tab_base[slot, b] = jnp.minimum(p0 + kk * CHUNK, limit)
            return b + 1

        return lax.fori_loop(0, nblk, per_block, nb)

    nb = jnp.int32(0)
    for e in range(N_EXPERTS):
        nb = per_expert(e, nb)
    return nb


def _dispatch_kernel(ptab_ref, sel_ref, h_ref, xs_ref, oh_ref, res_ref, tab_e, tab_base, tab_lo, cnt_ref, sem, *,
                     cap, tm):
    i = pl.program_id(0)
    last = pl.num_programs(0) - 1

    def chunk_copy(b):
        e = tab_e[0, b]
        base = tab_base[0, b]
        per = tm // CHUNK
        src = res_ref.at[b // per, pl.ds(pl.multiple_of(lax.rem(b, per) * CHUNK, CHUNK), CHUNK)]
        return pltpu.make_async_copy(src, xs_ref.at[e, pl.ds(base, CHUNK)], sem.at[0])

    def wait_all(nb):
        def w(b, c):
            chunk_copy(b).wait()
            return c
        lax.fori_loop(0, nb, w, 0)

    @pl.when(i == 0)
    def _():
        oh_ref[...] = jnp.zeros(oh_ref.shape, oh_ref.dtype)
        cnt_ref[0] = 0
        res_ref[0] = jnp.zeros(res_ref.shape[1:], res_ref.dtype)
        pads = [pltpu.make_async_copy(res_ref.at[0, pl.ds(0, CHUNK)], xs_ref.at[e, pl.ds(cap, CHUNK)], sem.at[0])
                for e in range(N_EXPERTS)]
        for c in pads:
            c.start()
        for c in pads:
            c.wait()

    wait_all(cnt_ref[0])
    nb = _block_table(ptab_ref, i, tab_e, tab_base, tab_lo, 0, cap)

    def build(b, c):
        e = tab_e[0, b]
        slots = tab_base[0, b] + lax.broadcasted_iota(jnp.int32, (CHUNK, tm), 0)
        hit = sel_ref[pl.ds(e, 1), :] == slots
        oh_ref[pl.ds(pl.multiple_of(b * CHUNK, CHUNK), CHUNK), :] = jnp.where(hit, 1.0, 0.0).astype(oh_ref.dtype)
        return c

    lax.fori_loop(0, nb, build, 0)

    def group(gi, c):
        rows = pl.ds(pl.multiple_of(gi * tm, tm), tm)
        res_ref[gi] = _dot(oh_ref[rows, :], h_ref[...]).astype(res_ref.dtype).reshape((tm,) + ROW_TILE)
        return c

    lax.fori_loop(0, jnp.right_shift(nb * CHUNK + tm - 1, tm.bit_length() - 1), group, 0)

    def issue(b, c):
        chunk_copy(b).start()
        return c

    lax.fori_loop(0, nb, issue, 0)
    cnt_ref[0] = nb

    @pl.when(i == last)
    def _():
        wait_all(nb)
        cnt_ref[0] = 0


def _dispatch(ptab, sel, h2, cap, tm):
    n = h2.shape[0]
    grid_spec = pltpu.PrefetchScalarGridSpec(
        num_scalar_prefetch=1,
        grid=(n // tm,),
        in_specs=[pl.BlockSpec((N_EXPERTS, tm), lambda i, p: (0, i)),
                  pl.BlockSpec((tm, D_MODEL), lambda i, p: (i, 0))],
        out_specs=pl.BlockSpec(memory_space=pl.ANY),
        scratch_shapes=[pltpu.VMEM((MAX_BLOCKS * CHUNK, tm), MXU_DTYPE),
                        pltpu.VMEM((MAX_BLOCKS * CHUNK // tm, tm) + ROW_TILE, MXU_DTYPE),
                        pltpu.SMEM((1, MAX_BLOCKS), jnp.int32), pltpu.SMEM((1, MAX_BLOCKS), jnp.int32),
                        pltpu.SMEM((1, MAX_BLOCKS), jnp.int32),
                        pltpu.SMEM((1,), jnp.int32), pltpu.SemaphoreType.DMA((1,))],
    )
    return pl.pallas_call(
        functools.partial(_dispatch_kernel, cap=cap, tm=tm),
        grid_spec=grid_spec,
        out_shape=jax.ShapeDtypeStruct((N_EXPERTS, cap + CHUNK) + ROW_TILE, MXU_DTYPE),
        compiler_params=pltpu.CompilerParams(dimension_semantics=("arbitrary",), vmem_limit_bytes=VMEM_LIMIT),
        name="moe_dispatch",
    )(ptab, sel, h2)


def _combine_kernel(ptab_ref, sel_ref, aff_ref, y_ref, ye_ref, o_ref, stage_ref, wt_ref, tab_e, tab_base, tab_lo,
                    cnt_ref, sem, *, cap, tm):
    i = pl.program_id(0)
    nt = pl.num_programs(0)
    slot = lax.rem(i, 2)

    def chunk_copy(s, b):
        e = tab_e[s, b]
        base = tab_base[s, b]
        dst = stage_ref.at[s, pl.ds(pl.multiple_of(b * CHUNK, CHUNK), CHUNK)]
        return pltpu.make_async_copy(ye_ref.at[e, pl.ds(base, CHUNK)], dst, sem.at[s])

    def fetch(tile, s):
        nb = _block_table(ptab_ref, tile, tab_e, tab_base, tab_lo, s, cap - CHUNK)
        cnt_ref[s] = nb

        def issue(b, c):
            chunk_copy(s, b).start()
            return c

        lax.fori_loop(0, nb, issue, 0)

    @pl.when(i == 0)
    def _():
        stage_ref[...] = jnp.zeros(stage_ref.shape, stage_ref.dtype)
        fetch(i, slot)

    @pl.when(i + 1 < nt)
    def _():
        fetch(i + 1, 1 - slot)

    nb = cnt_ref[slot]

    def wait(b, c):
        chunk_copy(slot, b).wait()
        return c

    lax.fori_loop(0, nb, wait, 0)
    ngroups = jnp.right_shift(nb * CHUNK + tm - 1, tm.bit_length() - 1)

    def build(b, c):
        e = tab_e[slot, b]
        slots = tab_base[slot, b] + lax.broadcasted_iota(jnp.int32, (CHUNK, tm), 0)
        hit = (sel_ref[pl.ds(e, 1), :] == slots) & (slots >= tab_lo[slot, b])
        w = jnp.where(hit, aff_ref[pl.ds(e, 1), :], 0.0)
        wt_ref[pl.ds(pl.multiple_of(b * CHUNK, CHUNK), CHUNK), :] = w.astype(wt_ref.dtype)
        return c

    def clear(b, c):
        wt_ref[pl.ds(pl.multiple_of(b * CHUNK, CHUNK), CHUNK), :] = jnp.zeros((CHUNK, tm), wt_ref.dtype)
        return c

    lax.fori_loop(0, nb, build, 0)
    lax.fori_loop(nb, ngroups * (tm // CHUNK), clear, 0)
    o_ref[...] = y_ref[...]

    def group(gi, c):
        rows = pl.ds(pl.multiple_of(gi * tm, tm), tm)
        o_ref[...] += lax.dot_general(wt_ref[rows, :], stage_ref[slot, rows].reshape(tm, D_MODEL),
                                      (((0,), (0,)), ((), ())), preferred_element_type=jnp.float32)
        return c

    lax.fori_loop(0, ngroups, group, 0)


def _combine(ptab, sel, aff, y, ye, cap, tm):
    n = y.shape[0]
    grid_spec = pltpu.PrefetchScalarGridSpec(
        num_scalar_prefetch=1,
        grid=(n // tm,),
        in_specs=[pl.BlockSpec((N_EXPERTS, tm), lambda i, p: (0, i)),
                  pl.BlockSpec((N_EXPERTS, tm), lambda i, p: (0, i)),
                  pl.BlockSpec((tm, D_MODEL), lambda i, p: (i, 0)),
                  pl.BlockSpec(memory_space=pl.ANY)],
        out_specs=pl.BlockSpec((tm, D_MODEL), lambda i, p: (i, 0)),
        scratch_shapes=[pltpu.VMEM((2, MAX_BLOCKS * CHUNK) + ROW_TILE, MXU_DTYPE),
                        pltpu.VMEM((MAX_BLOCKS * CHUNK, tm), MXU_DTYPE),
                        pltpu.SMEM((2, MAX_BLOCKS), jnp.int32), pltpu.SMEM((2, MAX_BLOCKS), jnp.int32),
                        pltpu.SMEM((2, MAX_BLOCKS), jnp.int32),
                        pltpu.SMEM((2,), jnp.int32), pltpu.SemaphoreType.DMA((2,))],
    )
    return pl.pallas_call(
        functools.partial(_combine_kernel, cap=cap, tm=tm),
        grid_spec=grid_spec,
        out_shape=jax.ShapeDtypeStruct((n, D_MODEL), jnp.float32),
        compiler_params=pltpu.CompilerParams(dimension_semantics=("arbitrary",), vmem_limit_bytes=VMEM_LIMIT),
        name="moe_combine",
    )(ptab, sel, aff, y, ye)


def _rope_tables(seq):
    pos = jnp.arange(seq, dtype=jnp.int32)
    inv = jnp.power(ROPE_THETA, -jnp.arange(0, 32, 2, dtype=jnp.float32) / 32.0)
    ang_row = (pos // GRID_W).astype(jnp.float32)[:, None] * inv[None, :]
    ang_col = (pos % GRID_W).astype(jnp.float32)[:, None] * inv[None, :]
    ang_seq = pos.astype(jnp.float32)[:, None] * inv[None, :]
    sign = jnp.concatenate([-jnp.ones((16,), jnp.float32), jnp.ones((16,), jnp.float32)])

    def pair(ang):
        return jnp.tile(jnp.cos(ang), (1, 2)), jnp.tile(jnp.sin(ang), (1, 2)) * sign[None, :]

    cr, sr = pair(ang_row)
    cc, sc = pair(ang_col)
    cs, ss = pair(ang_seq)
    one = jnp.ones((seq, 32), jnp.float32)
    zero = jnp.zeros((seq, 32), jnp.float32)
    return {
        "cosa": jnp.concatenate([cr, cc, cr, cc], axis=1),
        "sina": jnp.concatenate([sr, sc, sr, sc], axis=1),
        "cosb": jnp.concatenate([one, one, cs, one], axis=1),
        "sinb": jnp.concatenate([zero, zero, ss, zero], axis=1),
    }


def _prep_layer(l, p):
    f32 = jnp.float32
    w_in = p["w_in"][l]
    a0, b0, c0, x0, g0 = 0, 768, 1440, 2208, 2464

    kr = jnp.zeros((D_MODEL, LANES), f32).at[:, 64:96].set(w_in[:, b0 + 640:b0 + 672])
    w1 = jnp.concatenate([
        w_in[:, a0:a0 + 768],
        w_in[:, b0:b0 + 384], w_in[:, b0 + 384:b0 + 640],
        w_in[:, c0:c0 + 768],
        w_in[:, x0:x0 + 256], kr], axis=1)
    assert w1.shape[1] == _C_END

    wqb = jnp.zeros((B_Q_RANK, B_HEADS, LANES), f32).at[:, :, :B_QK].set(
        p["b_w_q_b"][l].reshape(B_Q_RANK, B_HEADS, B_QK)).reshape(B_Q_RANK, B_HEADS * LANES)
    wkv = p["b_w_kv_b"][l].reshape(B_KV_RANK, B_HEADS, 2 * B_NOPE)
    wkb = jnp.zeros((B_KV_RANK, B_HEADS, LANES), f32).at[:, :, :B_NOPE].set(
        wkv[:, :, :B_NOPE]).reshape(B_KV_RANK, B_HEADS * LANES)
    wvb = wkv[:, :, B_NOPE:].reshape(B_KV_RANK, B_HEADS * B_NOPE)

    def g64(g, scale=1.0):
        return jnp.tile(g * scale, 2)

    def g96(g, scale=1.0):
        return jnp.concatenate([g * scale, jnp.zeros((LANES - B_QK,), f32)])

    gains = jnp.stack([
        g64(p["a_q_norm"][l], HEAD ** -0.5 * LOG2E), g64(p["a_k_norm"][l]),
        g96(p["b_q_norm"][l], B_QK ** -0.5 * LOG2E), g96(p["b_k_norm"][l]),
        g64(p["c_q_norm"][l], HEAD ** -0.5 * LOG2E), g64(p["c_k_norm"][l]),
        g64(p["x_q_norm"][l], HEAD ** -0.5 * LOG2E), jnp.zeros((LANES,), f32)])

    wr = p["w_router"][l].T
    wrh = wr.astype(MXU_DTYPE)
    wrl = (wr - wrh.astype(f32)).astype(MXU_DTYPE)
    bf = lambda a: a.astype(MXU_DTYPE)
    return {
        "g1": p["norm1_g"][l][None, :], "w1": bf(w1), "wqb": bf(wqb), "wkb": bf(wkb), "wvb": bf(wvb),
        "gains": gains, "gbq": p["b_q_a_norm"][l][None, :], "gbkv": p["b_kv_a_norm"][l][None, :],
        "gmem": p["mem_norm_g"][l][None, :], "wmem": bf(p["w_mem_kv"][l]), "gxk": g64(p["x_k_norm"][l])[None, :],
        "sink": p["c_sink"][l],
        "wg": bf(w_in[:, g0:]), "wbr": bf(p["w_branch"][l]), "wo": bf(p["w_out"][l]),
        "g2": p["norm2_g"][l][None, :], "wrh": wrh, "wrl": wrl,
        "wgu": bf(p["w_gate_up"][l]), "wd": bf(p["w_down"][l]),
    }


def _trunk(x, mem, layers):
    batch, seq, _ = x.shape
    n = batch * seq
    tm = ROUTE_CHUNK
    tp = min(512, seq)
    cap = EC_FACTOR * n // N_EXPERTS
    tc = min(512, cap)
    tabs = _rope_tables(seq)
    x2 = x.reshape(n, D_MODEL)
    mem2 = mem.reshape(batch * N_MEM, D_MODEL)
    for lw in layers:
        qa, ka, va, qb, kb, vb, qc, kc, vc, qx = _in_projection(x2, lw, tabs, seq, tp)
        kx, vx = _memory_kv(mem2, lw)
        oa = _flash_attention(qa, ka, va, mode="gqa", batch=batch, seq=seq, kv_len=seq, tq=TQ_A, tk=TK_A)
        ob = _flash_attention(qb, kb, vb, mode="mha", batch=batch, seq=seq, kv_len=seq, tq=TQ_B, tk=TK_B)
        oc = _window_attention(qc, kc, vc, lw["sink"], batch=batch, seq=seq, tq=min(256, seq - 2 * C_WINDOW))
        ox = _flash_attention(qx, kx, vx, mode="cross", batch=batch, seq=seq, kv_len=N_MEM, tq=TQ_X, tk=N_MEM)
        y, h2, aff = _merge(x2, oa, ob, oc, ox, lw, tp)
        sel, ptab = _route(aff, cap)
        xs = _dispatch(ptab, sel, h2, cap, tm)
        ye = _expert_ffn(xs, lw["wgu"], lw["wd"], cap, tc)
        x2 = _combine(ptab, sel, aff, y, ye, cap, tm)
    return x2.reshape(batch, seq, D_MODEL)


def kernel(x_prompt, x_sample, mem_prompt, mem_sample, norm1_g, w_in, a_q_norm, a_k_norm, b_q_a_norm,
           b_w_q_b, b_kv_a_norm, b_w_kv_b, b_q_norm, b_k_norm, c_q_norm, c_k_norm, c_sink, x_q_norm,
           x_k_norm, mem_norm_g, w_mem_kv, w_branch, w_out, norm2_g, w_router, w_gate_up, w_down):
    p = dict(norm1_g=norm1_g, w_in=w_in, a_q_norm=a_q_norm, a_k_norm=a_k_norm, b_q_a_norm=b_q_a_norm,
             b_w_q_b=b_w_q_b, b_kv_a_norm=b_kv_a_norm, b_w_kv_b=b_w_kv_b, b_q_norm=b_q_norm, b_k_norm=b_k_norm,
             c_q_norm=c_q_norm, c_k_norm=c_k_norm, c_sink=c_sink, x_q_norm=x_q_norm, x_k_norm=x_k_norm,
             mem_norm_g=mem_norm_g, w_mem_kv=w_mem_kv, w_branch=w_branch, w_out=w_out, norm2_g=norm2_g,
             w_router=w_router, w_gate_up=w_gate_up, w_down=w_down)
    layers = [_prep_layer(l, p) for l in range(w_in.shape[0])]
    return (_trunk(x_prompt, mem_prompt, layers), _trunk(x_sample, mem_sample, layers))
```

```python
import functools

import jax
import jax.numpy as jnp
from jax import lax
from jax.experimental import pallas as pl
from jax.experimental.pallas import tpu as pltpu

D_MODEL = 1024
GRID_W = 64
N_MEM = 256
EPS = 1e-6
ROPE_THETA = 10000.0
NEG_INF = -1e30
HEAD = 64
A_KV_HEADS = 2
B_HEADS = 8
B_Q_RANK = 384
B_KV_RANK = 256
B_NOPE = 64
B_ROPE = 32
B_QK = B_NOPE + B_ROPE
C_WINDOW = 128
X_HEADS = 4
N_EXPERTS = 16
EC_FACTOR = 2
D_EXPERT = 1024
LANES = 128

MXU_DTYPE = jnp.bfloat16
VMEM_LIMIT = 56 * 1024 * 1024
LOG2E = 1.4426950408889634
TQ_A, TK_A = 1024, 512
TQ_B, TK_B = 2048, 512
TQ_X = 1024
FLASH_COLS = 256

_C_AQ, _C_AK, _C_AV = 0, 512, 640
_C_BCQ, _C_BCKV = 768, 1152
_C_CQ, _C_CK, _C_CV = 1408, 1920, 2048
_C_XQ, _C_KR, _C_END = 2176, 2432, 2560
V_ROWS = 80


def _dot(a, b):
    return jnp.dot(a, b, preferred_element_type=jnp.float32)


def _dot_nt(a, b):
    return lax.dot_general(a, b, (((1,), (1,)), ((), ())), preferred_element_type=jnp.float32)


def _rms(x, g):
    return x * lax.rsqrt(jnp.mean(x * x, axis=-1, keepdims=True) + EPS) * g


def _block_ones(head_w):
    w = 2 * LANES
    row = lax.broadcasted_iota(jnp.int32, (w, w), 0) // head_w
    col = lax.broadcasted_iota(jnp.int32, (w, w), 1) // head_w
    return jnp.where(row == col, 1.0, 0.0).astype(MXU_DTYPE)


def _head_norm(z, gain, ones, n_valid):
    width = z.shape[1]
    outs = []
    c = 0
    while c < width:
        w = min(2 * LANES, width - c)
        blk = z[:, c:c + w]
        ss = _dot((blk * blk).astype(MXU_DTYPE), ones[:w, :w])
        g = gain if w == LANES else jnp.concatenate([gain, gain], axis=1)
        normed = blk * lax.rsqrt(ss * (1.0 / n_valid) + EPS) * g
        outs += [normed[:, i * LANES:(i + 1) * LANES] for i in range(w // LANES)]
        c += w
    return outs


def _rope(blocks, cos, sin_signed):
    lane = lax.broadcasted_iota(jnp.int32, (1, LANES), 1)
    first = (lane % 32) < 16
    outs = []
    for blk in blocks:
        up = pltpu.roll(blk, LANES - 16, 1)
        dn = pltpu.roll(blk, 16, 1)
        outs.append(blk * cos + jnp.where(first, up, dn) * sin_signed)
    return outs


def _cat(blocks, dtype):
    return jnp.concatenate(blocks, axis=1).astype(dtype)


def _transposed_v(v, dtype):
    vt = v.T
    ones = jnp.ones((V_ROWS - HEAD, v.shape[0]), jnp.float32)
    parts = []
    for h in range(v.shape[1] // HEAD):
        parts += [vt[h * HEAD:(h + 1) * HEAD, :], ones]
    return jnp.concatenate(parts, axis=0).astype(dtype)


def _inproj_kernel(x_ref, g1_ref, w1_ref, wqb_ref, wkb_ref, wvb_ref, gains_ref, gbq_ref, gbkv_ref,
                   cosa_ref, sina_ref, cosb_ref, sinb_ref,
                   qa_ref, ka_ref, va_ref, qb_ref, kb_ref, vb_ref, qc_ref, kc_ref, vc_ref, qx_ref):
    x = x_ref[...]
    hb = _rms(x, g1_ref[...]).astype(MXU_DTYPE)
    z = _dot(hb, w1_ref[...])
    gains = gains_ref[...]
    cosa, sina = cosa_ref[...], sina_ref[...]
    cosb, sinb = cosb_ref[...], sinb_ref[...]
    od = qa_ref.dtype
    ones64, ones128 = _block_ones(HEAD), _block_ones(LANES)

    qa = _rope(_head_norm(z[:, _C_AQ:_C_AK], gains[0:1], ones64, HEAD), cosa, sina)
    qa_ref[...] = _cat(qa, od)
    ka = _rope(_head_norm(z[:, _C_AK:_C_AV], gains[1:2], ones64, HEAD), cosa, sina)
    ka_ref[...] = ka[0].astype(od)
    va_ref[...] = _transposed_v(z[:, _C_AV:_C_BCQ], od)

    cq = _rms(z[:, _C_BCQ:_C_BCKV], gbq_ref[...]).astype(MXU_DTYPE)
    ckv = _rms(z[:, _C_BCKV:_C_CQ], gbkv_ref[...]).astype(MXU_DTYPE)
    q8 = _dot(cq, wqb_ref[...])
    k8 = _dot(ckv, wkb_ref[...])
    kr = z[:, _C_KR:_C_END]
    k8 = k8 + jnp.concatenate([kr] * B_HEADS, axis=1)
    qb_ref[...] = _cat(_rope(_head_norm(q8, gains[2:3], ones128, B_QK), cosb, sinb), od)
    kb_ref[...] = _cat(_rope(_head_norm(k8, gains[3:4], ones128, B_QK), cosb, sinb), od)
    vb_ref[...] = _transposed_v(_dot(ckv, wvb_ref[...]), od)

    qc_ref[...] = _cat(_head_norm(z[:, _C_CQ:_C_CK], gains[4:5], ones64, HEAD), od)
    kc_ref[...] = _head_norm(z[:, _C_CK:_C_CV], gains[5:6], ones64, HEAD)[0].astype(od)
    vc_ref[...] = _transposed_v(z[:, _C_CV:_C_XQ], od)

    qx_ref[...] = _cat(_head_norm(z[:, _C_XQ:_C_KR], gains[6:7], ones64, HEAD), od)


def _in_projection(x2, lw, tabs, seq, tm):
    n = x2.shape[0]
    nt = seq // tm
    row = lambda i: (i, 0)
    full = lambda i: (0, 0)
    tab = lambda i: (i % nt, 0)
    col = lambda i: (0, i)
    outs_def = [(512, False), (LANES, False), (A_KV_HEADS * V_ROWS, True), (1024, False), (1024, False),
                (B_HEADS * V_ROWS, True), (512, False), (LANES, False), (A_KV_HEADS * V_ROWS, True), (256, False)]
    out_specs = [pl.BlockSpec((w, tm), col) if t else pl.BlockSpec((tm, w), row) for w, t in outs_def]
    out_shape = [jax.ShapeDtypeStruct((w, n) if t else (n, w), MXU_DTYPE) for w, t in outs_def]
    outs = pl.pallas_call(
        _inproj_kernel,
        grid=(n // tm,),
        in_specs=[
            pl.BlockSpec((tm, D_MODEL), row),
            pl.BlockSpec((1, D_MODEL), full),
            pl.BlockSpec(lw["w1"].shape, full),
            pl.BlockSpec(lw["wqb"].shape, full),
            pl.BlockSpec(lw["wkb"].shape, full),
            pl.BlockSpec(lw["wvb"].shape, full),
            pl.BlockSpec(lw["gains"].shape, full),
            pl.BlockSpec((1, B_Q_RANK), full),
            pl.BlockSpec((1, B_KV_RANK), full),
            pl.BlockSpec((tm, LANES), tab),
            pl.BlockSpec((tm, LANES), tab),
            pl.BlockSpec((tm, LANES), tab),
            pl.BlockSpec((tm, LANES), tab),
        ],
        out_specs=out_specs,
        out_shape=out_shape,
        compiler_params=pltpu.CompilerParams(dimension_semantics=("parallel",), vmem_limit_bytes=VMEM_LIMIT),
        name="in_projection",
    )(x2, lw["g1"], lw["w1"], lw["wqb"], lw["wkb"], lw["wvb"], lw["gains"], lw["gbq"], lw["gbkv"],
      tabs["cosa"], tabs["sina"], tabs["cosb"], tabs["sinb"])
    return outs


def _memkv_kernel(mem_ref, g_ref, w_ref, gain_ref, k_ref, v_ref):
    m = _rms(mem_ref[...], g_ref[...]).astype(MXU_DTYPE)
    kv = _dot(m, w_ref[...])
    half = kv.shape[1] // 2
    k_ref[...] = _cat(_head_norm(kv[:, :half], gain_ref[...], _block_ones(HEAD), HEAD), k_ref.dtype)
    v_ref[...] = _transposed_v(kv[:, half:], v_ref.dtype)


def _memory_kv(mem2, lw):
    n = mem2.shape[0]
    tm = N_MEM
    xq = X_HEADS * HEAD
    row = lambda i: (i, 0)
    full = lambda i: (0, 0)
    return pl.pallas_call(
        _memkv_kernel,
        grid=(n // tm,),
        in_specs=[pl.BlockSpec((tm, D_MODEL), row), pl.BlockSpec((1, D_MODEL), full),
                  pl.BlockSpec((D_MODEL, 2 * xq), full), pl.BlockSpec((1, LANES), full)],
        out_specs=[pl.BlockSpec((tm, xq), row), pl.BlockSpec((X_HEADS * V_ROWS, tm), lambda i: (0, i))],
        out_shape=[jax.ShapeDtypeStruct((n, xq), MXU_DTYPE),
                   jax.ShapeDtypeStruct((X_HEADS * V_ROWS, n), MXU_DTYPE)],
        compiler_params=pltpu.CompilerParams(dimension_semantics=("parallel",)),
        name="memory_kv",
    )(mem2, lw["gmem"], lw["wmem"], lw["gxk"])


def _flash_kernel(q_ref, k_ref, v_ref, o_ref, qs_ref, s_ref, m_ref, acc_ref, *, mode, nh, tq, tk, n_kv):
    g = pl.program_id(1)
    qt = q_ref[...].astype(jnp.float32).T
    kdim = qs_ref.shape[0]
    for i in range(nh):
        if mode == "gqa":
            blk = qt[i * HEAD:(i + 1) * HEAD, :]
            half = lax.broadcasted_iota(jnp.int32, (kdim, tq), 0) // HEAD
            op = jnp.where(half == g, jnp.concatenate([blk, blk], axis=0), 0.0)
        elif mode == "mha":
            op = qt[i * LANES:(i + 1) * LANES, :]
        else:
            head = lax.broadcasted_iota(jnp.int32, (kdim, tq), 0) // HEAD
            op = jnp.where(head == i, qt, 0.0)
        qs_ref[:, i * tq:(i + 1) * tq] = op.astype(qs_ref.dtype)
    m_ref[...] = jnp.full(m_ref.shape, -jnp.inf, jnp.float32)
    acc_ref[...] = jnp.zeros(acc_ref.shape, jnp.float32)

    ncols = nh * tq
    cw = min(tq if mode == "mha" else ncols, FLASH_COLS)
    chunks = [(c, c + cw) for c in range(0, ncols, cw)]

    def score(j, slot, c0, c1):
        off = pl.multiple_of(j * tk, tk)
        if mode == "mha":
            i = c0 // tq
            s_ref[slot, :, c0:c1] = _dot(k_ref[pl.ds(off, tk), i * LANES:(i + 1) * LANES], qs_ref[:, c0:c1])
        else:
            s_ref[slot, :, c0:c1] = _dot(k_ref[pl.ds(off, tk), :], qs_ref[:, c0:c1])

    def consume(j, slot, c0, c1):
        off = pl.multiple_of(j * tk, tk)
        s = s_ref[slot, :, c0:c1]
        m_prev = m_ref[0:1, c0:c1]
        m_new = jnp.maximum(m_prev, jnp.max(s, axis=0, keepdims=True))
        alpha = jnp.exp2(m_prev - m_new)
        p = jnp.exp2(s - m_new).astype(v_ref.dtype)
        if mode == "gqa":
            pv = _dot(v_ref[:, pl.ds(off, tk)], p)
        else:
            pv = jnp.concatenate(
                [_dot(v_ref[i * V_ROWS:(i + 1) * V_ROWS, pl.ds(off, tk)],
                      p[:, max(i * tq, c0) - c0:min((i + 1) * tq, c1) - c0])
                 for i in range(c0 // tq, (c1 + tq - 1) // tq)], axis=1)
        acc_ref[:, c0:c1] = alpha * acc_ref[:, c0:c1] + pv
        m_ref[0:1, c0:c1] = m_new

    def step(js, slot_s, jc, slot_c):
        for c0, c1 in chunks:
            if js is not None:
                score(js, slot_s, c0, c1)
            if jc is not None:
                consume(jc, slot_c, c0, c1)

    step(0, 0, None, None)
    if n_kv > 1:
        def pair(jj, carry):
            step(2 * jj + 1, 1, 2 * jj, 0)
            step(2 * jj + 2, 0, 2 * jj + 1, 1)
            return carry

        lax.fori_loop(0, n_kv // 2 - 1, pair, 0)
        step(n_kv - 1, 1, n_kv - 2, 0)
        step(None, None, n_kv - 1, 1)
    else:
        step(None, None, 0, 0)
    a = acc_ref[...]
    ot = a[:HEAD, :] / a[HEAD:HEAD + 1, :]
    o_ref[...] = jnp.concatenate([ot[:, i * tq:(i + 1) * tq] for i in range(nh)], axis=0).T.astype(o_ref.dtype)


def _flash_attention(q, k, vt, *, mode, batch, seq, kv_len, tq, tk):
    kw = 256
    groups = q.shape[1] // kw
    nh = {"gqa": 4, "mha": 2, "cross": 4}[mode]
    kcols = {"gqa": LANES, "mha": kw, "cross": kw}[mode]
    kdim = {"gqa": LANES, "mha": LANES, "cross": kw}[mode]
    vrows = V_ROWS if mode == "gqa" else nh * V_ROWS
    k_map = (lambda b, g, i: (b, 0)) if mode == "gqa" else (lambda b, g, i: (b, g))
    tq = min(tq, seq)
    tk = min(tk, kv_len)
    nq = seq // tq
    ow = nh * HEAD
    kern = functools.partial(_flash_kernel, mode=mode, nh=nh, tq=tq, tk=tk, n_kv=kv_len // tk)
    return pl.pallas_call(
        kern,
        grid=(batch, groups, nq),
        in_specs=[pl.BlockSpec((tq, kw), lambda b, g, i: (b * nq + i, g)),
                  pl.BlockSpec((kv_len, kcols), k_map),
                  pl.BlockSpec((vrows, kv_len), lambda b, g, i: (g, b))],
        out_specs=pl.BlockSpec((tq, ow), lambda b, g, i: (b * nq + i, g)),
        out_shape=jax.ShapeDtypeStruct((batch * seq, groups * ow), MXU_DTYPE),
        scratch_shapes=[pltpu.VMEM((kdim, nh * tq), MXU_DTYPE), pltpu.VMEM((2, tk, nh * tq), jnp.float32),
                        pltpu.VMEM((8, nh * tq), jnp.float32), pltpu.VMEM((V_ROWS, nh * tq), jnp.float32)],
        compiler_params=pltpu.CompilerParams(dimension_semantics=("parallel", "parallel", "parallel"),
                                             vmem_limit_bytes=VMEM_LIMIT),
        name=f"flash_attention_{mode}",
    )(q, k, vt)


def _window_kernel(sink_ref, q_ref, k_ref, v_ref, o_ref, *, tq, seq):
    j = pl.program_id(1)
    nh = 8
    win = tq + 2 * C_WINDOW
    start = pl.multiple_of(jnp.clip(j * tq - C_WINDOW, 0, seq - win), C_WINDOW)
    k = k_ref[pl.ds(start, win), :]
    qt = q_ref[...].astype(jnp.float32).T
    half = lax.broadcasted_iota(jnp.int32, (2 * HEAD, tq), 0) // HEAD
    spos = start + lax.broadcasted_iota(jnp.int32, (win, tq), 0)
    tpos = j * tq + lax.broadcasted_iota(jnp.int32, (win, tq), 1)
    rel = jnp.abs(spos - tpos)
    dist = jnp.where(rel <= C_WINDOW, rel.astype(jnp.float32), -NEG_INF)
    outs = []
    for g in range(A_KV_HEADS):
        vt = v_ref[g * V_ROWS:(g + 1) * V_ROWS, pl.ds(start, win)]
        ps, sinks = [], []
        for i in range(nh // A_KV_HEADS):
            h = g * (nh // A_KV_HEADS) + i
            blk = qt[h * HEAD:(h + 1) * HEAD, :]
            op = jnp.where(half == g, jnp.concatenate([blk, blk], axis=0), 0.0).astype(k.dtype)
            si = _dot(k, op) - (2.0 ** -(h + 1) * LOG2E) * dist
            sink = sink_ref[h] * LOG2E
            m = jnp.maximum(jnp.max(si, axis=0, keepdims=True), sink)
            ps.append(jnp.exp2(si - m).astype(vt.dtype))
            sinks.append(jnp.exp2(sink - m))
        acc = _dot(vt, jnp.concatenate(ps, axis=1))
        ot = acc[:HEAD, :] / (acc[HEAD:HEAD + 1, :] + jnp.concatenate(sinks, axis=1))
        outs += [ot[:, i * tq:(i + 1) * tq] for i in range(nh // A_KV_HEADS)]
    o_ref[...] = jnp.concatenate(outs, axis=0).T.astype(o_ref.dtype)


def _window_attention(q, k, v, sink, *, batch, seq, tq):
    width = q.shape[1]
    nq = seq // tq
    kern = functools.partial(_window_kernel, tq=tq, seq=seq)
    return pl.pallas_call(
        kern,
        grid=(batch, nq),
        in_specs=[pl.BlockSpec(memory_space=pltpu.SMEM),
                  pl.BlockSpec((tq, width), lambda b, i: (b * nq + i, 0)),
                  pl.BlockSpec((seq, LANES), lambda b, i: (b, 0)),
                  pl.BlockSpec((A_KV_HEADS * V_ROWS, seq), lambda b, i: (0, b))],
        out_specs=pl.BlockSpec((tq, width), lambda b, i: (b * nq + i, 0)),
        out_shape=jax.ShapeDtypeStruct((batch * seq, width), MXU_DTYPE),
        compiler_params=pltpu.CompilerParams(dimension_semantics=("parallel", "parallel"),
                                             vmem_limit_bytes=VMEM_LIMIT),
        name="window_attention",
    )(sink, q, k, v)


def _merge_kernel(x_ref, oa_ref, ob_ref, oc_ref, ox_ref, g1_ref, wg_ref, wbr_ref, wo_ref, g2_ref,
                  wrh_ref, wrl_ref, y_ref, h2_ref, aff_ref):
    x = x_ref[...]
    hb = _rms(x, g1_ref[...]).astype(MXU_DTYPE)
    merged = jnp.zeros(x.shape, jnp.float32)
    r0 = 0
    for i, o_ref in enumerate((oa_ref, ob_ref, oc_ref, ox_ref)):
        r1 = r0 + o_ref.shape[1]
        gate = jax.nn.sigmoid(_dot(hb, wg_ref[:, i * D_MODEL:(i + 1) * D_MODEL]))
        merged = merged + gate * _dot(o_ref[...], wbr_ref[r0:r1, :])
        r0 = r1
    y = x + _dot(merged.astype(MXU_DTYPE), wo_ref[...])
    y_ref[...] = y
    h2 = _rms(y, g2_ref[...])
    h2_hi = h2.astype(MXU_DTYPE)
    h2_lo = (h2 - h2_hi.astype(jnp.float32)).astype(MXU_DTYPE)
    h2_ref[...] = h2_hi
    wrh, wrl = wrh_ref[...], wrl_ref[...]
    logits = _dot_nt(wrh, h2_hi) + _dot_nt(wrh, h2_lo) + _dot_nt(wrl, h2_hi)
    e = jnp.exp(logits - jnp.max(logits, axis=0, keepdims=True))
    aff_ref[...] = e / jnp.sum(e, axis=0, keepdims=True)


def _merge(x2, oa, ob, oc, ox, lw, tm):
    n = x2.shape[0]
    row = lambda i: (i, 0)
    full = lambda i: (0, 0)
    return pl.pallas_call(
        _merge_kernel,
        grid=(n // tm,),
        in_specs=[pl.BlockSpec((tm, D_MODEL), row),
                  pl.BlockSpec((tm, oa.shape[1]), row), pl.BlockSpec((tm, ob.shape[1]), row),
                  pl.BlockSpec((tm, oc.shape[1]), row), pl.BlockSpec((tm, ox.shape[1]), row),
                  pl.BlockSpec((1, D_MODEL), full),
                  pl.BlockSpec(lw["wg"].shape, full, pipeline_mode=pl.Buffered(1)),
                  pl.BlockSpec(lw["wbr"].shape, full, pipeline_mode=pl.Buffered(1)),
                  pl.BlockSpec(lw["wo"].shape, full, pipeline_mode=pl.Buffered(1)), pl.BlockSpec((1, D_MODEL), full),
                  pl.BlockSpec(lw["wrh"].shape, full), pl.BlockSpec(lw["wrl"].shape, full)],
        out_specs=[pl.BlockSpec((tm, D_MODEL), row), pl.BlockSpec((tm, D_MODEL), row),
                   pl.BlockSpec((N_EXPERTS, tm), lambda i: (0, i))],
        out_shape=[jax.ShapeDtypeStruct((n, D_MODEL), jnp.float32),
                   jax.ShapeDtypeStruct((n, D_MODEL), MXU_DTYPE),
                   jax.ShapeDtypeStruct((N_EXPERTS, n), jnp.float32)],
        compiler_params=pltpu.CompilerParams(dimension_semantics=("parallel",), vmem_limit_bytes=VMEM_LIMIT),
        name="merge_router",
    )(x2, oa, ob, oc, ox, lw["g1"], lw["wg"], lw["wbr"], lw["wo"], lw["g2"], lw["wrh"], lw["wrl"])


ROW_TILE = (8, LANES)


def _expert_kernel(xs_ref, wgu_ref, wd_ref, y_ref, wgu_bf, wd_bf):
    tc = xs_ref.shape[1]

    @pl.when(pl.program_id(1) == 0)
    def _():
        wgu_bf[...] = wgu_ref[0].astype(wgu_bf.dtype)
        wd_bf[...] = wd_ref[0].astype(wd_bf.dtype)

    gu = _dot(xs_ref[0].reshape(tc, D_MODEL), wgu_bf[...])
    g, u = gu[:, :D_EXPERT], gu[:, D_EXPERT:]
    act = (g * jax.nn.sigmoid(g) * u).astype(MXU_DTYPE)
    y_ref[0] = _dot(act, wd_bf[...]).astype(y_ref.dtype).reshape((tc,) + ROW_TILE)


def _expert_ffn(xs, wgu, wd, cap, tc):
    nt = cap // tc
    return pl.pallas_call(
        _expert_kernel,
        grid=(N_EXPERTS, nt),
        in_specs=[pl.BlockSpec((1, tc) + ROW_TILE, lambda e, c: (e, c, 0, 0)),
                  pl.BlockSpec((1, D_MODEL, 2 * D_EXPERT), lambda e, c: (e, 0, 0)),
                  pl.BlockSpec((1, D_EXPERT, D_MODEL), lambda e, c: (e, 0, 0))],
        out_specs=pl.BlockSpec((1, tc) + ROW_TILE, lambda e, c: (e, c, 0, 0)),
        out_shape=jax.ShapeDtypeStruct((N_EXPERTS, cap) + ROW_TILE, MXU_DTYPE),
        scratch_shapes=[pltpu.VMEM((D_MODEL, 2 * D_EXPERT), MXU_DTYPE), pltpu.VMEM((D_EXPERT, D_MODEL), MXU_DTYPE)],
        compiler_params=pltpu.CompilerParams(dimension_semantics=("arbitrary", "arbitrary"),
                                             vmem_limit_bytes=VMEM_LIMIT),
        name="expert_ffn",
    )(xs, wgu, wd)


CHUNK = 32
MAX_BLOCKS = 128
ROUTE_CHUNK = 256


def _route_kernel(aff_ref, sel_ref, ptab_ref, *, cap, n):
    bits = lax.bitcast_convert_type(aff_ref[...], jnp.int32)

    def search(b, thr):
        cand = thr | jnp.left_shift(jnp.int32(1), 30 - b)
        cnt = jnp.sum((bits >= cand).astype(jnp.int32), axis=1, keepdims=True)
        return jnp.where(cnt >= cap, cand, thr)

    thr = lax.fori_loop(0, 31, search, jnp.zeros((N_EXPERTS, 1), jnp.int32))
    n_gt = jnp.sum((bits > thr).astype(jnp.int32), axis=1, keepdims=True)
    need = (cap - n_gt).astype(jnp.float32)
    k = lax.broadcasted_iota(jnp.int32, (ROUTE_CHUNK, ROUTE_CHUNK), 0)
    j = lax.broadcasted_iota(jnp.int32, (ROUTE_CHUNK, ROUTE_CHUNK), 1)
    tri = jnp.where(k <= j, 1.0, 0.0).astype(MXU_DTYPE)
    tile = lax.broadcasted_iota(jnp.int32, ptab_ref.shape, 1)

    def chunk(c, carry):
        cg, ce, tab = carry
        off = pl.multiple_of(c * ROUTE_CHUNK, ROUTE_CHUNK)
        tab = jnp.where(tile == c, cg + jnp.minimum(ce, need), tab)
        b = lax.bitcast_convert_type(aff_ref[:, pl.ds(off, ROUTE_CHUNK)], jnp.int32)
        gt = jnp.where(b > thr, 1.0, 0.0)
        eq = jnp.where(b == thr, 1.0, 0.0)
        pre = _dot(jnp.concatenate([gt, eq], axis=0).astype(MXU_DTYPE), tri)
        pg = pre[:N_EXPERTS] + cg
        pe = pre[N_EXPERTS:] + ce
        before = (pg - gt) + jnp.minimum(pe - eq, need)
        chosen = (gt > 0.0) | ((eq > 0.0) & ((pe - eq) < need))
        sel_ref[:, pl.ds(off, ROUTE_CHUNK)] = jnp.where(chosen, before.astype(jnp.int32), -1)
        return pg[:, ROUTE_CHUNK - 1:ROUTE_CHUNK], pe[:, ROUTE_CHUNK - 1:ROUTE_CHUNK], tab

    zero = jnp.zeros((N_EXPERTS, 1), jnp.float32)
    n_tiles = n // ROUTE_CHUNK
    _, _, tab = lax.fori_loop(0, n_tiles, chunk, (zero, zero, jnp.zeros(ptab_ref.shape, jnp.float32)))
    ptab_ref[...] = jnp.where(tile == n_tiles, cap, tab.astype(jnp.int32))


def _route(aff, cap):
    n = aff.shape[1]
    ntp = -(-(n // ROUTE_CHUNK + 1) // LANES) * LANES
    full = lambda i: (0, 0)
    return pl.pallas_call(
        functools.partial(_route_kernel, cap=cap, n=n),
        grid=(1,),
        in_specs=[pl.BlockSpec((N_EXPERTS, n), full)],
        out_specs=[pl.BlockSpec((N_EXPERTS, n), full), pl.BlockSpec((N_EXPERTS, ntp), full)],
        out_shape=[jax.ShapeDtypeStruct((N_EXPERTS, n), jnp.int32),
                   jax.ShapeDtypeStruct((N_EXPERTS, ntp), jnp.int32)],
        compiler_params=pltpu.CompilerParams(dimension_semantics=("arbitrary",), vmem_limit_bytes=VMEM_LIMIT),
        name="route_select",
    )(aff)


def _block_table(ptab_ref, i, tab_e, tab_base, tab_lo, slot, limit):
    def per_expert(e, nb):
        p0 = ptab_ref[e, i]
        nblk = jnp.right_shift(ptab_ref[e, i + 1] - p0 + (CHUNK - 1), 5)

        def per_block(kk, b):
            tab_e[slot, b] = e
            tab_lo[slot, b] = p0 + kk * CHUNK
            tab_base[slot, b] = jnp.minimum(p0 + kk * CHUNK, limit)
            return b + 1

        return lax.fori_loop(0, nblk, per_block, nb)

    nb = jnp.int32(0)
    for e in range(N_EXPERTS):
        nb = per_expert(e, nb)
    return nb


def _dispatch_kernel(ptab_ref, sel_ref, h_ref, xs_ref, oh_ref, res_ref, tab_e, tab_base, tab_lo, cnt_ref, sem, *,
                     cap, tm):
    i = pl.program_id(0)
    last = pl.num_programs(0) - 1

    def chunk_copy(b):
        e = tab_e[0, b]
        base = tab_base[0, b]
        per = tm // CHUNK
        src = res_ref.at[b // per, pl.ds(pl.multiple_of(lax.rem(b, per) * CHUNK, CHUNK), CHUNK)]
        return pltpu.make_async_copy(src, xs_ref.at[e, pl.ds(base, CHUNK)], sem.at[0])

    def wait_all(nb):
        def w(b, c):
            chunk_copy(b).wait()
            return c
        lax.fori_loop(0, nb, w, 0)

    @pl.when(i == 0)
    def _():
        oh_ref[...] = jnp.zeros(oh_ref.shape, oh_ref.dtype)
        cnt_ref[0] = 0
        res_ref[0] = jnp.zeros(res_ref.shape[1:], res_ref.dtype)
        pads = [pltpu.make_async_copy(res_ref.at[0, pl.ds(0, CHUNK)], xs_ref.at[e, pl.ds(cap, CHUNK)], sem.at[0])
                for e in range(N_EXPERTS)]
        for c in pads:
            c.start()
        for c in pads:
            c.wait()

    wait_all(cnt_ref[0])
    nb = _block_table(ptab_ref, i, tab_e, tab_base, tab_lo, 0, cap)

    def build(b, c):
        e = tab_e[0, b]
        slots = tab_base[0, b] + lax.broadcasted_iota(jnp.int32, (CHUNK, tm), 0)
        hit = sel_ref[pl.ds(e, 1), :] == slots
        oh_ref[pl.ds(pl.multiple_of(b * CHUNK, CHUNK), CHUNK), :] = jnp.where(hit, 1.0, 0.0).astype(oh_ref.dtype)
        return c

    lax.fori_loop(0, nb, build, 0)

    def group(gi, c):
        rows = pl.ds(pl.multiple_of(gi * tm, tm), tm)
        res_ref[gi] = _dot(oh_ref[rows, :], h_ref[...]).astype(res_ref.dtype).reshape((tm,) + ROW_TILE)
        return c

    lax.fori_loop(0, jnp.right_shift(nb * CHUNK + tm - 1, tm.bit_length() - 1), group, 0)

    def issue(b, c):
        chunk_copy(b).start()
        return c

    lax.fori_loop(0, nb, issue, 0)
    cnt_ref[0] = nb

    @pl.when(i == last)
    def _():
        wait_all(nb)
        cnt_ref[0] = 0


def _dispatch(ptab, sel, h2, cap, tm):
    n = h2.shape[0]
    grid_spec = pltpu.PrefetchScalarGridSpec(
        num_scalar_prefetch=1,
        grid=(n // tm,),
        in_specs=[pl.BlockSpec((N_EXPERTS, tm), lambda i, p: (0, i)),
                  pl.BlockSpec((tm, D_MODEL), lambda i, p: (i, 0))],
        out_specs=pl.BlockSpec(memory_space=pl.ANY),
        scratch_shapes=[pltpu.VMEM((MAX_BLOCKS * CHUNK, tm), MXU_DTYPE),
                        pltpu.VMEM((MAX_BLOCKS * CHUNK // tm, tm) + ROW_TILE, MXU_DTYPE),
                        pltpu.SMEM((1, MAX_BLOCKS), jnp.int32), pltpu.SMEM((1, MAX_BLOCKS), jnp.int32),
                        pltpu.SMEM((1, MAX_BLOCKS), jnp.int32),
                        pltpu.SMEM((1,), jnp.int32), pltpu.SemaphoreType.DMA((1,))],
    )
    return pl.pallas_call(
        functools.partial(_dispatch_kernel, cap=cap, tm=tm),
        grid_spec=grid_spec,
        out_shape=jax.ShapeDtypeStruct((N_EXPERTS, cap + CHUNK) + ROW_TILE, MXU_DTYPE),
        compiler_params=pltpu.CompilerParams(dimension_semantics=("arbitrary",), vmem_limit_bytes=VMEM_LIMIT),
        name="moe_dispatch",
    )(ptab, sel, h2)


def _combine_kernel(ptab_ref, sel_ref, aff_ref, y_ref, ye_ref, o_ref, stage_ref, wt_ref, tab_e, tab_base, tab_lo,
                    cnt_ref, sem, *, cap, tm):
    i = pl.program_id(0)
    nt = pl.num_programs(0)
    slot = lax.rem(i, 2)

    def chunk_copy(s, b):
        e = tab_e[s, b]
        base = tab_base[s, b]
        dst = stage_ref.at[s, pl.ds(pl.multiple_of(b * CHUNK, CHUNK), CHUNK)]
        return pltpu.make_async_copy(ye_ref.at[e, pl.ds(base, CHUNK)], dst, sem.at[s])

    def fetch(tile, s):
        nb = _block_table(ptab_ref, tile, tab_e, tab_base, tab_lo, s, cap - CHUNK)
        cnt_ref[s] = nb

        def issue(b, c):
            chunk_copy(s, b).start()
            return c

        lax.fori_loop(0, nb, issue, 0)

    @pl.when(i == 0)
    def _():
        stage_ref[...] = jnp.zeros(stage_ref.shape, stage_ref.dtype)
        fetch(i, slot)

    @pl.when(i + 1 < nt)
    def _():
        fetch(i + 1, 1 - slot)

    nb = cnt_ref[slot]

    def wait(b, c):
        chunk_copy(slot, b).wait()
        return c

    lax.fori_loop(0, nb, wait, 0)
    ngroups = jnp.right_shift(nb * CHUNK + tm - 1, tm.bit_length() - 1)

    def build(b, c):
        e = tab_e[slot, b]
        slots = tab_base[slot, b] + lax.broadcasted_iota(jnp.int32, (CHUNK, tm), 0)
        hit = (sel_ref[pl.ds(e, 1), :] == slots) & (slots >= tab_lo[slot, b])
        w = jnp.where(hit, aff_ref[pl.ds(e, 1), :], 0.0)
        wt_ref[pl.ds(pl.multiple_of(b * CHUNK, CHUNK), CHUNK), :] = w.astype(wt_ref.dtype)
        return c

    def clear(b, c):
        wt_ref[pl.ds(pl.multiple_of(b * CHUNK, CHUNK), CHUNK), :] = jnp.zeros((CHUNK, tm), wt_ref.dtype)
        return c

    lax.fori_loop(0, nb, build, 0)
    lax.fori_loop(nb, ngroups * (tm // CHUNK), clear, 0)
    o_ref[...] = y_ref[...]

    def group(gi, c):
        rows = pl.ds(pl.multiple_of(gi * tm, tm), tm)
        o_ref[...] += lax.dot_general(wt_ref[rows, :], stage_ref[slot, rows].reshape(tm, D_MODEL),
                                      (((0,), (0,)), ((), ())), preferred_element_type=jnp.float32)
        return c

    lax.fori_loop(0, ngroups, group, 0)


def _combine(ptab, sel, aff, y, ye, cap, tm):
    n = y.shape[0]
    grid_spec = pltpu.PrefetchScalarGridSpec(
        num_scalar_prefetch=1,
        grid=(n // tm,),
        in_specs=[pl.BlockSpec((N_EXPERTS, tm), lambda i, p: (0, i)),
                  pl.BlockSpec((N_EXPERTS, tm), lambda i, p: (0, i)),
                  pl.BlockSpec((tm, D_MODEL), lambda i, p: (i, 0)),
                  pl.BlockSpec(memory_space=pl.ANY)],
        out_specs=pl.BlockSpec((tm, D_MODEL), lambda i, p: (i, 0)),
        scratch_shapes=[pltpu.VMEM((2, MAX_BLOCKS * CHUNK) + ROW_TILE, MXU_DTYPE),
                        pltpu.VMEM((MAX_BLOCKS * CHUNK, tm), MXU_DTYPE),
                        pltpu.SMEM((2, MAX_BLOCKS), jnp.int32), pltpu.SMEM((2, MAX_BLOCKS), jnp.int32),
                        pltpu.SMEM((2, MAX_BLOCKS), jnp.int32),
                        pltpu.SMEM((2,), jnp.int32), pltpu.SemaphoreType.DMA((2,))],
    )
    return pl.pallas_call(
        functools.partial(_combine_kernel, cap=cap, tm=tm),
        grid_spec=grid_spec,
        out_shape=jax.ShapeDtypeStruct((n, D_MODEL), jnp.float32),
        compiler_params=pltpu.CompilerParams(dimension_semantics=("arbitrary",), vmem_limit_bytes=VMEM_LIMIT),
        name="moe_combine",
    )(ptab, sel, aff, y, ye)


def _rope_tables(seq):
    pos = jnp.arange(seq, dtype=jnp.int32)
    inv = jnp.power(ROPE_THETA, -jnp.arange(0, 32, 2, dtype=jnp.float32) / 32.0)
    ang_row = (pos // GRID_W).astype(jnp.float32)[:, None] * inv[None, :]
    ang_col = (pos % GRID_W).astype(jnp.float32)[:, None] * inv[None, :]
    ang_seq = pos.astype(jnp.float32)[:, None] * inv[None, :]
    sign = jnp.concatenate([-jnp.ones((16,), jnp.float32), jnp.ones((16,), jnp.float32)])

    def pair(ang):
        return jnp.tile(jnp.cos(ang), (1, 2)), jnp.tile(jnp.sin(ang), (1, 2)) * sign[None, :]

    cr, sr = pair(ang_row)
    cc, sc = pair(ang_col)
    cs, ss = pair(ang_seq)
    one = jnp.ones((seq, 32), jnp.float32)
    zero = jnp.zeros((seq, 32), jnp.float32)
    return {
        "cosa": jnp.concatenate([cr, cc, cr, cc], axis=1),
        "sina": jnp.concatenate([sr, sc, sr, sc], axis=1),
        "cosb": jnp.concatenate([one, one, cs, one], axis=1),
        "sinb": jnp.concatenate([zero, zero, ss, zero], axis=1),
    }


def _prep_layer(l, p):
    f32 = jnp.float32
    w_in = p["w_in"][l]
    a0, b0, c0, x0, g0 = 0, 768, 1440, 2208, 2464

    kr = jnp.zeros((D_MODEL, LANES), f32).at[:, 64:96].set(w_in[:, b0 + 640:b0 + 672])
    w1 = jnp.concatenate([
        w_in[:, a0:a0 + 768],
        w_in[:, b0:b0 + 384], w_in[:, b0 + 384:b0 + 640],
        w_in[:, c0:c0 + 768],
        w_in[:, x0:x0 + 256], kr], axis=1)
    assert w1.shape[1] == _C_END

    wqb = jnp.zeros((B_Q_RANK, B_HEADS, LANES), f32).at[:, :, :B_QK].set(
        p["b_w_q_b"][l].reshape(B_Q_RANK, B_HEADS, B_QK)).reshape(B_Q_RANK, B_HEADS * LANES)
    wkv = p["b_w_kv_b"][l].reshape(B_KV_RANK, B_HEADS, 2 * B_NOPE)
    wkb = jnp.zeros((B_KV_RANK, B_HEADS, LANES), f32).at[:, :, :B_NOPE].set(
        wkv[:, :, :B_NOPE]).reshape(B_KV_RANK, B_HEADS * LANES)
    wvb = wkv[:, :, B_NOPE:].reshape(B_KV_RANK, B_HEADS * B_NOPE)

    def g64(g, scale=1.0):
        return jnp.tile(g * scale, 2)

    def g96(g, scale=1.0):
        return jnp.concatenate([g * scale, jnp.zeros((LANES - B_QK,), f32)])

    gains = jnp.stack([
        g64(p["a_q_norm"][l], HEAD ** -0.5 * LOG2E), g64(p["a_k_norm"][l]),
        g96(p["b_q_norm"][l], B_QK ** -0.5 * LOG2E), g96(p["b_k_norm"][l]),
        g64(p["c_q_norm"][l], HEAD ** -0.5 * LOG2E), g64(p["c_k_norm"][l]),
        g64(p["x_q_norm"][l], HEAD ** -0.5 * LOG2E), jnp.zeros((LANES,), f32)])

    wr = p["w_router"][l].T
    wrh = wr.astype(MXU_DTYPE)
    wrl = (wr - wrh.astype(f32)).astype(MXU_DTYPE)
    bf = lambda a: a.astype(MXU_DTYPE)
    return {
        "g1": p["norm1_g"][l][None, :], "w1": bf(w1), "wqb": bf(wqb), "wkb": bf(wkb), "wvb": bf(wvb),
        "gains": gains, "gbq": p["b_q_a_norm"][l][None, :], "gbkv": p["b_kv_a_norm"][l][None, :],
        "gmem": p["mem_norm_g"][l][None, :], "wmem": bf(p["w_mem_kv"][l]), "gxk": g64(p["x_k_norm"][l])[None, :],
        "sink": p["c_sink"][l],
        "wg": bf(w_in[:, g0:]), "wbr": bf(p["w_branch"][l]), "wo": bf(p["w_out"][l]),
        "g2": p["norm2_g"][l][None, :], "wrh": wrh, "wrl": wrl,
        "wgu": p["w_gate_up"][l], "wd": p["w_down"][l],
    }


def _trunk(x, mem, layers):
    batch, seq, _ = x.shape
    n = batch * seq
    tm = ROUTE_CHUNK
    tp = min(512, seq)
    cap = EC_FACTOR * n // N_EXPERTS
    tc = min(512, cap)
    tabs = _rope_tables(seq)
    x2 = x.reshape(n, D_MODEL)
    mem2 = mem.reshape(batch * N_MEM, D_MODEL)
    for lw in layers:
        qa, ka, va, qb, kb, vb, qc, kc, vc, qx = _in_projection(x2, lw, tabs, seq, tp)
        kx, vx = _memory_kv(mem2, lw)
        oa = _flash_attention(qa, ka, va, mode="gqa", batch=batch, seq=seq, kv_len=seq, tq=TQ_A, tk=TK_A)
        ob = _flash_attention(qb, kb, vb, mode="mha", batch=batch, seq=seq, kv_len=seq, tq=TQ_B, tk=TK_B)
        oc = _window_attention(qc, kc, vc, lw["sink"], batch=batch, seq=seq, tq=min(256, seq - 2 * C_WINDOW))
        ox = _flash_attention(qx, kx, vx, mode="cross", batch=batch, seq=seq, kv_len=N_MEM, tq=TQ_X, tk=N_MEM)
        y, h2, aff = _merge(x2, oa, ob, oc, ox, lw, tp)
        sel, ptab = _route(aff, cap)
        xs = _dispatch(ptab, sel, h2, cap, tm)
        ye = _expert_ffn(xs, lw["wgu"], lw["wd"], cap, tc)
        x2 = _combine(ptab, sel, aff, y, ye, cap, tm)
    return x2.reshape(batch, seq, D_MODEL)


def kernel(x_prompt, x_sample, mem_prompt, mem_sample, norm1_g, w_in, a_q_norm, a_k_norm, b_q_a_norm,
           b_w_q_b, b_kv_a_norm, b_w_kv_b, b_q_norm, b_k_norm, c_q_norm, c_k_norm, c_sink, x_q_norm,
           x_k_norm, mem_norm_g, w_mem_kv, w_branch, w_out, norm2_g, w_router, w_gate_up, w_down):
    p = dict(norm1_g=norm1_g, w_in=w_in, a_q_norm=a_q_norm, a_k_norm=a_k_norm, b_q_a_norm=b_q_a_norm,
             b_w_q_b=b_w_q_b, b_kv_a_norm=b_kv_a_norm, b_w_kv_b=b_w_kv_b, b_q_norm=b_q_norm, b_k_norm=b_k_norm,
             c_q_norm=c_q_norm, c_k_norm=c_k_norm, c_sink=c_sink, x_q_norm=x_q_norm, x_k_norm=x_k_norm,
             mem_norm_g=mem_norm_g, w_mem_kv=w_mem_kv, w_branch=w_branch, w_out=w_out, norm2_g=norm2_g,
             w_router=w_router, w_gate_up=w_gate_up, w_down=w_down)
    layers = [_prep_layer(l, p) for l in range(w_in.shape[0])]
    return (_trunk(x_prompt, mem_prompt, layers), _trunk(x_sample, mem_sample, layers))
```

```python
import functools

import jax
import jax.numpy as jnp
from jax import lax
from jax.experimental import pallas as pl
from jax.experimental.pallas import tpu as pltpu

D_MODEL = 1024
GRID_W = 64
N_MEM = 256
EPS = 1e-6
ROPE_THETA = 10000.0
NEG_INF = -1e30
HEAD = 64
A_KV_HEADS = 2
B_HEADS = 8
B_Q_RANK = 384
B_KV_RANK = 256
B_NOPE = 64
B_ROPE = 32
B_QK = B_NOPE + B_ROPE
C_WINDOW = 128
X_HEADS = 4
N_EXPERTS = 16
EC_FACTOR = 2
D_EXPERT = 1024
LANES = 128

MXU_DTYPE = jnp.bfloat16
VMEM_LIMIT = 56 * 1024 * 1024
LOG2E = 1.4426950408889634
TQ_A, TK_A = 1024, 512
TQ_B, TK_B = 2048, 512
TQ_X = 1024
FLASH_COLS = 256

_C_AQ, _C_AK, _C_AV = 0, 512, 640
_C_BCQ, _C_BCKV = 768, 1152
_C_CQ, _C_CK, _C_CV = 1408, 1920, 2048
_C_XQ, _C_KR, _C_END = 2176, 2432, 2560
V_ROWS = 80


def _dot(a, b):
    return jnp.dot(a, b, preferred_element_type=jnp.float32)


def _dot_nt(a, b):
    return lax.dot_general(a, b, (((1,), (1,)), ((), ())), preferred_element_type=jnp.float32)


def _rms(x, g):
    return x * lax.rsqrt(jnp.mean(x * x, axis=-1, keepdims=True) + EPS) * g


def _block_ones(head_w):
    w = 2 * LANES
    row = lax.broadcasted_iota(jnp.int32, (w, w), 0) // head_w
    col = lax.broadcasted_iota(jnp.int32, (w, w), 1) // head_w
    return jnp.where(row == col, 1.0, 0.0).astype(MXU_DTYPE)


def _head_norm(z, gain, ones, n_valid):
    width = z.shape[1]
    outs = []
    c = 0
    while c < width:
        w = min(2 * LANES, width - c)
        blk = z[:, c:c + w]
        ss = _dot((blk * blk).astype(MXU_DTYPE), ones[:w, :w])
        g = gain if w == LANES else jnp.concatenate([gain, gain], axis=1)
        normed = blk * lax.rsqrt(ss * (1.0 / n_valid) + EPS) * g
        outs += [normed[:, i * LANES:(i + 1) * LANES] for i in range(w // LANES)]
        c += w
    return outs


def _rope(blocks, cos, sin_signed):
    lane = lax.broadcasted_iota(jnp.int32, (1, LANES), 1)
    first = (lane % 32) < 16
    outs = []
    for blk in blocks:
        up = pltpu.roll(blk, LANES - 16, 1)
        dn = pltpu.roll(blk, 16, 1)
        outs.append(blk * cos + jnp.where(first, up, dn) * sin_signed)
    return outs


def _cat(blocks, dtype):
    return jnp.concatenate(blocks, axis=1).astype(dtype)


def _transposed_v(v, dtype):
    vt = v.T
    ones = jnp.ones((V_ROWS - HEAD, v.shape[0]), jnp.float32)
    parts = []
    for h in range(v.shape[1] // HEAD):
        parts += [vt[h * HEAD:(h + 1) * HEAD, :], ones]
    return jnp.concatenate(parts, axis=0).astype(dtype)


def _inproj_kernel(x_ref, g1_ref, w1_ref, wqb_ref, wkb_ref, wvb_ref, gains_ref, gbq_ref, gbkv_ref,
                   cosa_ref, sina_ref, cosb_ref, sinb_ref,
                   qa_ref, ka_ref, va_ref, qb_ref, kb_ref, vb_ref, qc_ref, kc_ref, vc_ref, qx_ref):
    x = x_ref[...]
    hb = _rms(x, g1_ref[...]).astype(MXU_DTYPE)
    z = _dot(hb, w1_ref[...])
    gains = gains_ref[...]
    cosa, sina = cosa_ref[...], sina_ref[...]
    cosb, sinb = cosb_ref[...], sinb_ref[...]
    od = qa_ref.dtype
    ones64, ones128 = _block_ones(HEAD), _block_ones(LANES)

    qa = _rope(_head_norm(z[:, _C_AQ:_C_AK], gains[0:1], ones64, HEAD), cosa, sina)
    qa_ref[...] = _cat(qa, od)
    ka = _rope(_head_norm(z[:, _C_AK:_C_AV], gains[1:2], ones64, HEAD), cosa, sina)
    ka_ref[...] = ka[0].astype(od)
    va_ref[...] = _transposed_v(z[:, _C_AV:_C_BCQ], od)

    cq = _rms(z[:, _C_BCQ:_C_BCKV], gbq_ref[...]).astype(MXU_DTYPE)
    ckv = _rms(z[:, _C_BCKV:_C_CQ], gbkv_ref[...]).astype(MXU_DTYPE)
    q8 = _dot(cq, wqb_ref[...])
    k8 = _dot(ckv, wkb_ref[...])
    kr = z[:, _C_KR:_C_END]
    k8 = k8 + jnp.concatenate([kr] * B_HEADS, axis=1)
    qb_ref[...] = _cat(_rope(_head_norm(q8, gains[2:3], ones128, B_QK), cosb, sinb), od)
    kb_ref[...] = _cat(_rope(_head_norm(k8, gains[3:4], ones128, B_QK), cosb, sinb), od)
    vb_ref[...] = _transposed_v(_dot(ckv, wvb_ref[...]), od)

    qc_ref[...] = _cat(_head_norm(z[:, _C_CQ:_C_CK], gains[4:5], ones64, HEAD), od)
    kc_ref[...] = _head_norm(z[:, _C_CK:_C_CV], gains[5:6], ones64, HEAD)[0].astype(od)
    vc_ref[...] = _transposed_v(z[:, _C_CV:_C_XQ], od)

    qx_ref[...] = _cat(_head_norm(z[:, _C_XQ:_C_KR], gains[6:7], ones64, HEAD), od)


def _in_projection(x2, lw, tabs, seq, tm):
    n = x2.shape[0]
    nt = seq // tm
    row = lambda i: (i, 0)
    full = lambda i: (0, 0)
    tab = lambda i: (i % nt, 0)
    col = lambda i: (0, i)
    outs_def = [(512, False), (LANES, False), (A_KV_HEADS * V_ROWS, True), (1024, False), (1024, False),
                (B_HEADS * V_ROWS, True), (512, False), (LANES, False), (A_KV_HEADS * V_ROWS, True), (256, False)]
    out_specs = [pl.BlockSpec((w, tm), col) if t else pl.BlockSpec((tm, w), row) for w, t in outs_def]
    out_shape = [jax.ShapeDtypeStruct((w, n) if t else (n, w), MXU_DTYPE) for w, t in outs_def]
    outs = pl.pallas_call(
        _inproj_kernel,
        grid=(n // tm,),
        in_specs=[
            pl.BlockSpec((tm, D_MODEL), row),
            pl.BlockSpec((1, D_MODEL), full),
            pl.BlockSpec(lw["w1"].shape, full),
            pl.BlockSpec(lw["wqb"].shape, full),
            pl.BlockSpec(lw["wkb"].shape, full),
            pl.BlockSpec(lw["wvb"].shape, full),
            pl.BlockSpec(lw["gains"].shape, full),
            pl.BlockSpec((1, B_Q_RANK), full),
            pl.BlockSpec((1, B_KV_RANK), full),
            pl.BlockSpec((tm, LANES), tab),
            pl.BlockSpec((tm, LANES), tab),
            pl.BlockSpec((tm, LANES), tab),
            pl.BlockSpec((tm, LANES), tab),
        ],
        out_specs=out_specs,
        out_shape=out_shape,
        compiler_params=pltpu.CompilerParams(dimension_semantics=("parallel",), vmem_limit_bytes=VMEM_LIMIT),
        name="in_projection",
    )(x2, lw["g1"], lw["w1"], lw["wqb"], lw["wkb"], lw["wvb"], lw["gains"], lw["gbq"], lw["gbkv"],
      tabs["cosa"], tabs["sina"], tabs["cosb"], tabs["sinb"])
    return outs


def _memkv_kernel(mem_ref, g_ref, w_ref, gain_ref, k_ref, v_ref):
    m = _rms(mem_ref[...], g_ref[...]).astype(MXU_DTYPE)
    kv = _dot(m, w_ref[...])
    half = kv.shape[1] // 2
    k_ref[...] = _cat(_head_norm(kv[:, :half], gain_ref[...], _block_ones(HEAD), HEAD), k_ref.dtype)
    v_ref[...] = _transposed_v(kv[:, half:], v_ref.dtype)


def _memory_kv(mem2, lw):
    n = mem2.shape[0]
    tm = N_MEM
    xq = X_HEADS * HEAD
    row = lambda i: (i, 0)
    full = lambda i: (0, 0)
    return pl.pallas_call(
        _memkv_kernel,
        grid=(n // tm,),
        in_specs=[pl.BlockSpec((tm, D_MODEL), row), pl.BlockSpec((1, D_MODEL), full),
                  pl.BlockSpec((D_MODEL, 2 * xq), full), pl.BlockSpec((1, LANES), full)],
        out_specs=[pl.BlockSpec((tm, xq), row), pl.BlockSpec((X_HEADS * V_ROWS, tm), lambda i: (0, i))],
        out_shape=[jax.ShapeDtypeStruct((n, xq), MXU_DTYPE),
                   jax.ShapeDtypeStruct((X_HEADS * V_ROWS, n), MXU_DTYPE)],
        compiler_params=pltpu.CompilerParams(dimension_semantics=("parallel",)),
        name="memory_kv",
    )(mem2, lw["gmem"], lw["wmem"], lw["gxk"])


def _flash_kernel(q_ref, k_ref, v_ref, o_ref, qs_ref, s_ref, m_ref, acc_ref, *, mode, nh, tq, tk, n_kv):
    g = pl.program_id(1)
    qt = q_ref[...].astype(jnp.float32).T
    kdim = qs_ref.shape[0]
    for i in range(nh):
        if mode == "gqa":
            blk = qt[i * HEAD:(i + 1) * HEAD, :]
            half = lax.broadcasted_iota(jnp.int32, (kdim, tq), 0) // HEAD
            op = jnp.where(half == g, jnp.concatenate([blk, blk], axis=0), 0.0)
        elif mode == "mha":
            op = qt[i * LANES:(i + 1) * LANES, :]
        else:
            head = lax.broadcasted_iota(jnp.int32, (kdim, tq), 0) // HEAD
            op = jnp.where(head == i, qt, 0.0)
        qs_ref[:, i * tq:(i + 1) * tq] = op.astype(qs_ref.dtype)
    m_ref[...] = jnp.full(m_ref.shape, -jnp.inf, jnp.float32)
    acc_ref[...] = jnp.zeros(acc_ref.shape, jnp.float32)

    ncols = nh * tq
    cw = min(tq if mode == "mha" else ncols, FLASH_COLS)
    chunks = [(c, c + cw) for c in range(0, ncols, cw)]

    def score(j, slot, c0, c1):
        off = pl.multiple_of(j * tk, tk)
        if mode == "mha":
            i = c0 // tq
            s_ref[slot, :, c0:c1] = _dot(k_ref[pl.ds(off, tk), i * LANES:(i + 1) * LANES], qs_ref[:, c0:c1])
        else:
            s_ref[slot, :, c0:c1] = _dot(k_ref[pl.ds(off, tk), :], qs_ref[:, c0:c1])

    def consume(j, slot, c0, c1):
        off = pl.multiple_of(j * tk, tk)
        s = s_ref[slot, :, c0:c1]
        m_prev = m_ref[0:1, c0:c1]
        m_new = jnp.maximum(m_prev, jnp.max(s, axis=0, keepdims=True))
        alpha = jnp.exp2(m_prev - m_new)
        p = jnp.exp2(s - m_new).astype(v_ref.dtype)
        if mode == "gqa":
            pv = _dot(v_ref[:, pl.ds(off, tk)], p)
        else:
            pv = jnp.concatenate(
                [_dot(v_ref[i * V_ROWS:(i + 1) * V_ROWS, pl.ds(off, tk)],
                      p[:, max(i * tq, c0) - c0:min((i + 1) * tq, c1) - c0])
                 for i in range(c0 // tq, (c1 + tq - 1) // tq)], axis=1)
        acc_ref[:, c0:c1] = alpha * acc_ref[:, c0:c1] + pv
        m_ref[0:1, c0:c1] = m_new

    def step(js, slot_s, jc, slot_c):
        for c0, c1 in chunks:
            if js is not None:
                score(js, slot_s, c0, c1)
            if jc is not None:
                consume(jc, slot_c, c0, c1)

    step(0, 0, None, None)
    if n_kv > 1:
        def pair(jj, carry):
            step(2 * jj + 1, 1, 2 * jj, 0)
            step(2 * jj + 2, 0, 2 * jj + 1, 1)
            return carry

        lax.fori_loop(0, n_kv // 2 - 1, pair, 0)
        step(n_kv - 1, 1, n_kv - 2, 0)
        step(None, None, n_kv - 1, 1)
    else:
        step(None, None, 0, 0)
    a = acc_ref[...]
    ot = a[:HEAD, :] / a[HEAD:HEAD + 1, :]
    o_ref[...] = jnp.concatenate([ot[:, i * tq:(i + 1) * tq] for i in range(nh)], axis=0).T.astype(o_ref.dtype)


def _flash_attention(q, k, vt, *, mode, batch, seq, kv_len, tq, tk):
    kw = 256
    groups = q.shape[1] // kw
    nh = {"gqa": 4, "mha": 2, "cross": 4}[mode]
    kcols = {"gqa": LANES, "mha": kw, "cross": kw}[mode]
    kdim = {"gqa": LANES, "mha": LANES, "cross": kw}[mode]
    vrows = V_ROWS if mode == "gqa" else nh * V_ROWS
    k_map = (lambda b, g, i: (b, 0)) if mode == "gqa" else (lambda b, g, i: (b, g))
    tq = min(tq, seq)
    tk = min(tk, kv_len)
    nq = seq // tq
    ow = nh * HEAD
    kern = functools.partial(_flash_kernel, mode=mode, nh=nh, tq=tq, tk=tk, n_kv=kv_len // tk)
    return pl.pallas_call(
        kern,
        grid=(batch, groups, nq),
        in_specs=[pl.BlockSpec((tq, kw), lambda b, g, i: (b * nq + i, g)),
                  pl.BlockSpec((kv_len, kcols), k_map),
                  pl.BlockSpec((vrows, kv_len), lambda b, g, i: (g, b))],
        out_specs=pl.BlockSpec((tq, ow), lambda b, g, i: (b * nq + i, g)),
        out_shape=jax.ShapeDtypeStruct((batch * seq, groups * ow), MXU_DTYPE),
        scratch_shapes=[pltpu.VMEM((kdim, nh * tq), MXU_DTYPE), pltpu.VMEM((2, tk, nh * tq), jnp.float32),
                        pltpu.VMEM((8, nh * tq), jnp.float32), pltpu.VMEM((V_ROWS, nh * tq), jnp.float32)],
        compiler_params=pltpu.CompilerParams(dimension_semantics=("parallel", "parallel", "parallel"),
                                             vmem_limit_bytes=VMEM_LIMIT),
        name=f"flash_attention_{mode}",
    )(q, k, vt)


def _window_kernel(sink_ref, q_ref, k_ref, v_ref, o_ref, *, tq, seq):
    j = pl.program_id(1)
    nh = 8
    win = tq + 2 * C_WINDOW
    start = pl.multiple_of(jnp.clip(j * tq - C_WINDOW, 0, seq - win), C_WINDOW)
    k = k_ref[pl.ds(start, win), :]
    qt = q_ref[...].astype(jnp.float32).T
    half = lax.broadcasted_iota(jnp.int32, (2 * HEAD, tq), 0) // HEAD
    spos = start + lax.broadcasted_iota(jnp.int32, (win, tq), 0)
    tpos = j * tq + lax.broadcasted_iota(jnp.int32, (win, tq), 1)
    rel = jnp.abs(spos - tpos)
    dist = jnp.where(rel <= C_WINDOW, rel.astype(jnp.float32), -NEG_INF)
    outs = []
    for g in range(A_KV_HEADS):
        vt = v_ref[g * V_ROWS:(g + 1) * V_ROWS, pl.ds(start, win)]
        ps, sinks = [], []
        for i in range(nh // A_KV_HEADS):
            h = g * (nh // A_KV_HEADS) + i
            blk = qt[h * HEAD:(h + 1) * HEAD, :]
            op = jnp.where(half == g, jnp.concatenate([blk, blk], axis=0), 0.0).astype(k.dtype)
            si = _dot(k, op) - (2.0 ** -(h + 1) * LOG2E) * dist
            sink = sink_ref[h] * LOG2E
            m = jnp.maximum(jnp.max(si, axis=0, keepdims=True), sink)
            ps.append(jnp.exp2(si - m).astype(vt.dtype))
            sinks.append(jnp.exp2(sink - m))
        acc = _dot(vt, jnp.concatenate(ps, axis=1))
        ot = acc[:HEAD, :] / (acc[HEAD:HEAD + 1, :] + jnp.concatenate(sinks, axis=1))
        outs += [ot[:, i * tq:(i + 1) * tq] for i in range(nh // A_KV_HEADS)]
    o_ref[...] = jnp.concatenate(outs, axis=0).T.astype(o_ref.dtype)


def _window_attention(q, k, v, sink, *, batch, seq, tq):
    width = q.shape[1]
    nq = seq // tq
    kern = functools.partial(_window_kernel, tq=tq, seq=seq)
    return pl.pallas_call(
        kern,
        grid=(batch, nq),
        in_specs=[pl.BlockSpec(memory_space=pltpu.SMEM),
                  pl.BlockSpec((tq, width), lambda b, i: (b * nq + i, 0)),
                  pl.BlockSpec((seq, LANES), lambda b, i: (b, 0)),
                  pl.BlockSpec((A_KV_HEADS * V_ROWS, seq), lambda b, i: (0, b))],
        out_specs=pl.BlockSpec((tq, width), lambda b, i: (b * nq + i, 0)),
        out_shape=jax.ShapeDtypeStruct((batch * seq, width), MXU_DTYPE),
        compiler_params=pltpu.CompilerParams(dimension_semantics=("parallel", "parallel"),
                                             vmem_limit_bytes=VMEM_LIMIT),
        name="window_attention",
    )(sink, q, k, v)


def _merge_kernel(x_ref, oa_ref, ob_ref, oc_ref, ox_ref, g1_ref, wg_ref, wbr_ref, wo_ref, g2_ref,
                  wrh_ref, wrl_ref, y_ref, h2_ref, aff_ref):
    x = x_ref[...]
    hb = _rms(x, g1_ref[...]).astype(MXU_DTYPE)
    merged = jnp.zeros(x.shape, jnp.float32)
    r0 = 0
    for i, o_ref in enumerate((oa_ref, ob_ref, oc_ref, ox_ref)):
        r1 = r0 + o_ref.shape[1]
        gate = jax.nn.sigmoid(_dot(hb, wg_ref[:, i * D_MODEL:(i + 1) * D_MODEL]))
        merged = merged + gate * _dot(o_ref[...], wbr_ref[r0:r1, :])
        r0 = r1
    y = x + _dot(merged.astype(MXU_DTYPE), wo_ref[...])
    y_ref[...] = y
    h2 = _rms(y, g2_ref[...])
    h2_hi = h2.astype(MXU_DTYPE)
    h2_lo = (h2 - h2_hi.astype(jnp.float32)).astype(MXU_DTYPE)
    h2_ref[...] = h2_hi
    wrh, wrl = wrh_ref[...], wrl_ref[...]
    logits = _dot_nt(wrh, h2_hi) + _dot_nt(wrh, h2_lo) + _dot_nt(wrl, h2_hi)
    e = jnp.exp(logits - jnp.max(logits, axis=0, keepdims=True))
    aff_ref[...] = e / jnp.sum(e, axis=0, keepdims=True)


def _merge(x2, oa, ob, oc, ox, lw, tm):
    n = x2.shape[0]
    row = lambda i: (i, 0)
    full = lambda i: (0, 0)
    return pl.pallas_call(
        _merge_kernel,
        grid=(n // tm,),
        in_specs=[pl.BlockSpec((tm, D_MODEL), row),
                  pl.BlockSpec((tm, oa.shape[1]), row), pl.BlockSpec((tm, ob.shape[1]), row),
                  pl.BlockSpec((tm, oc.shape[1]), row), pl.BlockSpec((tm, ox.shape[1]), row),
                  pl.BlockSpec((1, D_MODEL), full),
                  pl.BlockSpec(lw["wg"].shape, full, pipeline_mode=pl.Buffered(1)),
                  pl.BlockSpec(lw["wbr"].shape, full, pipeline_mode=pl.Buffered(1)),
                  pl.BlockSpec(lw["wo"].shape, full, pipeline_mode=pl.Buffered(1)), pl.BlockSpec((1, D_MODEL), full),
                  pl.BlockSpec(lw["wrh"].shape, full), pl.BlockSpec(lw["wrl"].shape, full)],
        out_specs=[pl.BlockSpec((tm, D_MODEL), row), pl.BlockSpec((tm, D_MODEL), row),
                   pl.BlockSpec((N_EXPERTS, tm), lambda i: (0, i))],
        out_shape=[jax.ShapeDtypeStruct((n, D_MODEL), jnp.float32),
                   jax.ShapeDtypeStruct((n, D_MODEL), MXU_DTYPE),
                   jax.ShapeDtypeStruct((N_EXPERTS, n), jnp.float32)],
        compiler_params=pltpu.CompilerParams(dimension_semantics=("parallel",), vmem_limit_bytes=VMEM_LIMIT),
        name="merge_router",
    )(x2, oa, ob, oc, ox, lw["g1"], lw["wg"], lw["wbr"], lw["wo"], lw["g2"], lw["wrh"], lw["wrl"])


ROW_TILE = (8, LANES)


def _expert_kernel(xs_ref, wgu_ref, wd_ref, y_ref, wgu_bf, wd_bf):
    tc = xs_ref.shape[1]

    @pl.when(pl.program_id(1) == 0)
    def _():
        wgu_bf[...] = wgu_ref[0, 0].astype(wgu_bf.dtype)
        wd_bf[...] = wd_ref[0, 0].astype(wd_bf.dtype)

    gu = _dot(xs_ref[0].reshape(tc, D_MODEL), wgu_bf[...])
    g, u = gu[:, :D_EXPERT], gu[:, D_EXPERT:]
    act = (g * jax.nn.sigmoid(g) * u).astype(MXU_DTYPE)
    y_ref[0] = _dot(act, wd_bf[...]).astype(y_ref.dtype).reshape((tc,) + ROW_TILE)


def _expert_ffn(xs, wgu, wd, layer, cap, tc):
    nt = cap // tc
    return pl.pallas_call(
        _expert_kernel,
        grid=(N_EXPERTS, nt),
        in_specs=[pl.BlockSpec((1, tc) + ROW_TILE, lambda e, c: (e, c, 0, 0)),
                  pl.BlockSpec((1, 1, D_MODEL, 2 * D_EXPERT), lambda e, c: (layer, e, 0, 0)),
                  pl.BlockSpec((1, 1, D_EXPERT, D_MODEL), lambda e, c: (layer, e, 0, 0))],
        out_specs=pl.BlockSpec((1, tc) + ROW_TILE, lambda e, c: (e, c, 0, 0)),
        out_shape=jax.ShapeDtypeStruct((N_EXPERTS, cap) + ROW_TILE, MXU_DTYPE),
        scratch_shapes=[pltpu.VMEM((D_MODEL, 2 * D_EXPERT), MXU_DTYPE), pltpu.VMEM((D_EXPERT, D_MODEL), MXU_DTYPE)],
        compiler_params=pltpu.CompilerParams(dimension_semantics=("arbitrary", "arbitrary"),
                                             vmem_limit_bytes=VMEM_LIMIT),
        name="expert_ffn",
    )(xs, wgu, wd)


CHUNK = 32
MAX_BLOCKS = 128
ROUTE_CHUNK = 256


def _route_kernel(aff_ref, sel_ref, ptab_ref, *, cap, n):
    bits = lax.bitcast_convert_type(aff_ref[...], jnp.int32)

    def search(b, thr):
        cand = thr | jnp.left_shift(jnp.int32(1), 30 - b)
        cnt = jnp.sum((bits >= cand).astype(jnp.int32), axis=1, keepdims=True)
        return jnp.where(cnt >= cap, cand, thr)

    thr = lax.fori_loop(0, 31, search, jnp.zeros((N_EXPERTS, 1), jnp.int32))
    n_gt = jnp.sum((bits > thr).astype(jnp.int32), axis=1, keepdims=True)
    need = (cap - n_gt).astype(jnp.float32)
    k = lax.broadcasted_iota(jnp.int32, (ROUTE_CHUNK, ROUTE_CHUNK), 0)
    j = lax.broadcasted_iota(jnp.int32, (ROUTE_CHUNK, ROUTE_CHUNK), 1)
    tri = jnp.where(k <= j, 1.0, 0.0).astype(MXU_DTYPE)
    tile = lax.broadcasted_iota(jnp.int32, ptab_ref.shape, 1)

    def chunk(c, carry):
        cg, ce, tab = carry
        off = pl.multiple_of(c * ROUTE_CHUNK, ROUTE_CHUNK)
        tab = jnp.where(tile == c, cg + jnp.minimum(ce, need), tab)
        b = lax.bitcast_convert_type(aff_ref[:, pl.ds(off, ROUTE_CHUNK)], jnp.int32)
        gt = jnp.where(b > thr, 1.0, 0.0)
        eq = jnp.where(b == thr, 1.0, 0.0)
        pre = _dot(jnp.concatenate([gt, eq], axis=0).astype(MXU_DTYPE), tri)
        pg = pre[:N_EXPERTS] + cg
        pe = pre[N_EXPERTS:] + ce
        before = (pg - gt) + jnp.minimum(pe - eq, need)
        chosen = (gt > 0.0) | ((eq > 0.0) & ((pe - eq) < need))
        sel_ref[:, pl.ds(off, ROUTE_CHUNK)] = jnp.where(chosen, before.astype(jnp.int32), -1)
        return pg[:, ROUTE_CHUNK - 1:ROUTE_CHUNK], pe[:, ROUTE_CHUNK - 1:ROUTE_CHUNK], tab

    zero = jnp.zeros((N_EXPERTS, 1), jnp.float32)
    n_tiles = n // ROUTE_CHUNK
    _, _, tab = lax.fori_loop(0, n_tiles, chunk, (zero, zero, jnp.zeros(ptab_ref.shape, jnp.float32)))
    ptab_ref[...] = jnp.where(tile == n_tiles, cap, tab.astype(jnp.int32))


def _route(aff, cap):
    n = aff.shape[1]
    ntp = -(-(n // ROUTE_CHUNK + 1) // LANES) * LANES
    full = lambda i: (0, 0)
    return pl.pallas_call(
        functools.partial(_route_kernel, cap=cap, n=n),
        grid=(1,),
        in_specs=[pl.BlockSpec((N_EXPERTS, n), full)],
        out_specs=[pl.BlockSpec((N_EXPERTS, n), full), pl.BlockSpec((N_EXPERTS, ntp), full)],
        out_shape=[jax.ShapeDtypeStruct((N_EXPERTS, n), jnp.int32),
                   jax.ShapeDtypeStruct((N_EXPERTS, ntp), jnp.int32)],
        compiler_params=pltpu.CompilerParams(dimension_semantics=("arbitrary",), vmem_limit_bytes=VMEM_LIMIT),
        name="route_select",
    )(aff)


def _block_table(ptab_ref, i, tab_e, tab_base, tab_lo, slot, limit):
    def per_expert(e, nb):
        p0 = ptab_ref[e, i]
        nblk = jnp.right_shift(ptab_ref[e, i + 1] - p0 + (CHUNK - 1), 5)

        def per_block(kk, b):
            tab_e[slot, b] = e
            tab_lo[slot, b] = p0 + kk * CHUNK
            tab_base[slot, b] = jnp.minimum(p0 + kk * CHUNK, limit)
            return b + 1

        return lax.fori_loop(0, nblk, per_block, nb)

    nb = jnp.int32(0)
    for e in range(N_EXPERTS):
        nb = per_expert(e, nb)
    return nb


def _dispatch_kernel(ptab_ref, sel_ref, h_ref, xs_ref, oh_ref, res_ref, tab_e, tab_base, tab_lo, cnt_ref, sem, *,
                     cap, tm):
    i = pl.program_id(0)
    last = pl.num_programs(0) - 1

    def chunk_copy(b):
        e = tab_e[0, b]
        base = tab_base[0, b]
        per = tm // CHUNK
        src = res_ref.at[b // per, pl.ds(pl.multiple_of(lax.rem(b, per) * CHUNK, CHUNK), CHUNK)]
        return pltpu.make_async_copy(src, xs_ref.at[e, pl.ds(base, CHUNK)], sem.at[0])

    def wait_all(nb):
        def w(b, c):
            chunk_copy(b).wait()
            return c
        lax.fori_loop(0, nb, w, 0)

    @pl.when(i == 0)
    def _():
        oh_ref[...] = jnp.zeros(oh_ref.shape, oh_ref.dtype)
        cnt_ref[0] = 0
        res_ref[0] = jnp.zeros(res_ref.shape[1:], res_ref.dtype)
        pads = [pltpu.make_async_copy(res_ref.at[0, pl.ds(0, CHUNK)], xs_ref.at[e, pl.ds(cap, CHUNK)], sem.at[0])
                for e in range(N_EXPERTS)]
        for c in pads:
            c.start()
        for c in pads:
            c.wait()

    wait_all(cnt_ref[0])
    nb = _block_table(ptab_ref, i, tab_e, tab_base, tab_lo, 0, cap)

    def build(b, c):
        e = tab_e[0, b]
        slots = tab_base[0, b] + lax.broadcasted_iota(jnp.int32, (CHUNK, tm), 0)
        hit = sel_ref[pl.ds(e, 1), :] == slots
        oh_ref[pl.ds(pl.multiple_of(b * CHUNK, CHUNK), CHUNK), :] = jnp.where(hit, 1.0, 0.0).astype(oh_ref.dtype)
        return c

    lax.fori_loop(0, nb, build, 0)

    def group(gi, c):
        rows = pl.ds(pl.multiple_of(gi * tm, tm), tm)
        res_ref[gi] = _dot(oh_ref[rows, :], h_ref[...]).astype(res_ref.dtype).reshape((tm,) + ROW_TILE)
        return c

    lax.fori_loop(0, jnp.right_shift(nb * CHUNK + tm - 1, tm.bit_length() - 1), group, 0)

    def issue(b, c):
        chunk_copy(b).start()
        return c

    lax.fori_loop(0, nb, issue, 0)
    cnt_ref[0] = nb

    @pl.when(i == last)
    def _():
        wait_all(nb)
        cnt_ref[0] = 0


def _dispatch(ptab, sel, h2, cap, tm):
    n = h2.shape[0]
    grid_spec = pltpu.PrefetchScalarGridSpec(
        num_scalar_prefetch=1,
        grid=(n // tm,),
        in_specs=[pl.BlockSpec((N_EXPERTS, tm), lambda i, p: (0, i)),
                  pl.BlockSpec((tm, D_MODEL), lambda i, p: (i, 0))],
        out_specs=pl.BlockSpec(memory_space=pl.ANY),
        scratch_shapes=[pltpu.VMEM((MAX_BLOCKS * CHUNK, tm), MXU_DTYPE),
                        pltpu.VMEM((MAX_BLOCKS * CHUNK // tm, tm) + ROW_TILE, MXU_DTYPE),
                        pltpu.SMEM((1, MAX_BLOCKS), jnp.int32), pltpu.SMEM((1, MAX_BLOCKS), jnp.int32),
                        pltpu.SMEM((1, MAX_BLOCKS), jnp.int32),
                        pltpu.SMEM((1,), jnp.int32), pltpu.SemaphoreType.DMA((1,))],
    )
    return pl.pallas_call(
        functools.partial(_dispatch_kernel, cap=cap, tm=tm),
        grid_spec=grid_spec,
        out_shape=jax.ShapeDtypeStruct((N_EXPERTS, cap + CHUNK) + ROW_TILE, MXU_DTYPE),
        compiler_params=pltpu.CompilerParams(dimension_semantics=("arbitrary",), vmem_limit_bytes=VMEM_LIMIT),
        name="moe_dispatch",
    )(ptab, sel, h2)


def _combine_kernel(ptab_ref, sel_ref, aff_ref, y_ref, ye_ref, o_ref, stage_ref, wt_ref, tab_e, tab_base, tab_lo,
                    cnt_ref, sem, *, cap, tm):
    i = pl.program_id(0)
    nt = pl.num_programs(0)
    slot = lax.rem(i, 2)

    def chunk_copy(s, b):
        e = tab_e[s, b]
        base = tab_base[s, b]
        dst = stage_ref.at[s, pl.ds(pl.multiple_of(b * CHUNK, CHUNK), CHUNK)]
        return pltpu.make_async_copy(ye_ref.at[e, pl.ds(base, CHUNK)], dst, sem.at[s])

    def fetch(tile, s):
        nb = _block_table(ptab_ref, tile, tab_e, tab_base, tab_lo, s, cap - CHUNK)
        cnt_ref[s] = nb

        def issue(b, c):
            chunk_copy(s, b).start()
            return c

        lax.fori_loop(0, nb, issue, 0)

    @pl.when(i == 0)
    def _():
        stage_ref[...] = jnp.zeros(stage_ref.shape, stage_ref.dtype)
        fetch(i, slot)

    @pl.when(i + 1 < nt)
    def _():
        fetch(i + 1, 1 - slot)

    nb = cnt_ref[slot]

    def wait(b, c):
        chunk_copy(slot, b).wait()
        return c

    lax.fori_loop(0, nb, wait, 0)
    ngroups = jnp.right_shift(nb * CHUNK + tm - 1, tm.bit_length() - 1)

    def build(b, c):
        e = tab_e[slot, b]
        slots = tab_base[slot, b] + lax.broadcasted_iota(jnp.int32, (CHUNK, tm), 0)
        hit = (sel_ref[pl.ds(e, 1), :] == slots) & (slots >= tab_lo[slot, b])
        w = jnp.where(hit, aff_ref[pl.ds(e, 1), :], 0.0)
        wt_ref[pl.ds(pl.multiple_of(b * CHUNK, CHUNK), CHUNK), :] = w.astype(wt_ref.dtype)
        return c

    def clear(b, c):
        wt_ref[pl.ds(pl.multiple_of(b * CHUNK, CHUNK), CHUNK), :] = jnp.zeros((CHUNK, tm), wt_ref.dtype)
        return c

    lax.fori_loop(0, nb, build, 0)
    lax.fori_loop(nb, ngroups * (tm // CHUNK), clear, 0)
    o_ref[...] = y_ref[...]

    def group(gi, c):
        rows = pl.ds(pl.multiple_of(gi * tm, tm), tm)
        o_ref[...] += lax.dot_general(wt_ref[rows, :], stage_ref[slot, rows].reshape(tm, D_MODEL),
                                      (((0,), (0,)), ((), ())), preferred_element_type=jnp.float32)
        return c

    lax.fori_loop(0, ngroups, group, 0)


def _combine(ptab, sel, aff, y, ye, cap, tm):
    n = y.shape[0]
    grid_spec = pltpu.PrefetchScalarGridSpec(
        num_scalar_prefetch=1,
        grid=(n // tm,),
        in_specs=[pl.BlockSpec((N_EXPERTS, tm), lambda i, p: (0, i)),
                  pl.BlockSpec((N_EXPERTS, tm), lambda i, p: (0, i)),
                  pl.BlockSpec((tm, D_MODEL), lambda i, p: (i, 0)),
                  pl.BlockSpec(memory_space=pl.ANY)],
        out_specs=pl.BlockSpec((tm, D_MODEL), lambda i, p: (i, 0)),
        scratch_shapes=[pltpu.VMEM((2, MAX_BLOCKS * CHUNK) + ROW_TILE, MXU_DTYPE),
                        pltpu.VMEM((MAX_BLOCKS * CHUNK, tm), MXU_DTYPE),
                        pltpu.SMEM((2, MAX_BLOCKS), jnp.int32), pltpu.SMEM((2, MAX_BLOCKS), jnp.int32),
                        pltpu.SMEM((2, MAX_BLOCKS), jnp.int32),
                        pltpu.SMEM((2,), jnp.int32), pltpu.SemaphoreType.DMA((2,))],
    )
    return pl.pallas_call(
        functools.partial(_combine_kernel, cap=cap, tm=tm),
        grid_spec=grid_spec,
        out_shape=jax.ShapeDtypeStruct((n, D_MODEL), jnp.float32),
        compiler_params=pltpu.CompilerParams(dimension_semantics=("arbitrary",), vmem_limit_bytes=VMEM_LIMIT),
        name="moe_combine",
    )(ptab, sel, aff, y, ye)


def _rope_tables(seq):
    pos = jnp.arange(seq, dtype=jnp.int32)
    inv = jnp.power(ROPE_THETA, -jnp.arange(0, 32, 2, dtype=jnp.float32) / 32.0)
    ang_row = (pos // GRID_W).astype(jnp.float32)[:, None] * inv[None, :]
    ang_col = (pos % GRID_W).astype(jnp.float32)[:, None] * inv[None, :]
    ang_seq = pos.astype(jnp.float32)[:, None] * inv[None, :]
    sign = jnp.concatenate([-jnp.ones((16,), jnp.float32), jnp.ones((16,), jnp.float32)])

    def pair(ang):
        return jnp.tile(jnp.cos(ang), (1, 2)), jnp.tile(jnp.sin(ang), (1, 2)) * sign[None, :]

    cr, sr = pair(ang_row)
    cc, sc = pair(ang_col)
    cs, ss = pair(ang_seq)
    one = jnp.ones((seq, 32), jnp.float32)
    zero = jnp.zeros((seq, 32), jnp.float32)
    return {
        "cosa": jnp.concatenate([cr, cc, cr, cc], axis=1),
        "sina": jnp.concatenate([sr, sc, sr, sc], axis=1),
        "cosb": jnp.concatenate([one, one, cs, one], axis=1),
        "sinb": jnp.concatenate([zero, zero, ss, zero], axis=1),
    }


def _prep_layer(l, p):
    f32 = jnp.float32
    w_in = p["w_in"][l]
    a0, b0, c0, x0, g0 = 0, 768, 1440, 2208, 2464

    kr = jnp.zeros((D_MODEL, LANES), f32).at[:, 64:96].set(w_in[:, b0 + 640:b0 + 672])
    w1 = jnp.concatenate([
        w_in[:, a0:a0 + 768],
        w_in[:, b0:b0 + 384], w_in[:, b0 + 384:b0 + 640],
        w_in[:, c0:c0 + 768],
        w_in[:, x0:x0 + 256], kr], axis=1)
    assert w1.shape[1] == _C_END

    wqb = jnp.zeros((B_Q_RANK, B_HEADS, LANES), f32).at[:, :, :B_QK].set(
        p["b_w_q_b"][l].reshape(B_Q_RANK, B_HEADS, B_QK)).reshape(B_Q_RANK, B_HEADS * LANES)
    wkv = p["b_w_kv_b"][l].reshape(B_KV_RANK, B_HEADS, 2 * B_NOPE)
    wkb = jnp.zeros((B_KV_RANK, B_HEADS, LANES), f32).at[:, :, :B_NOPE].set(
        wkv[:, :, :B_NOPE]).reshape(B_KV_RANK, B_HEADS * LANES)
    wvb = wkv[:, :, B_NOPE:].reshape(B_KV_RANK, B_HEADS * B_NOPE)

    def g64(g, scale=1.0):
        return jnp.tile(g * scale, 2)

    def g96(g, scale=1.0):
        return jnp.concatenate([g * scale, jnp.zeros((LANES - B_QK,), f32)])

    gains = jnp.stack([
        g64(p["a_q_norm"][l], HEAD ** -0.5 * LOG2E), g64(p["a_k_norm"][l]),
        g96(p["b_q_norm"][l], B_QK ** -0.5 * LOG2E), g96(p["b_k_norm"][l]),
        g64(p["c_q_norm"][l], HEAD ** -0.5 * LOG2E), g64(p["c_k_norm"][l]),
        g64(p["x_q_norm"][l], HEAD ** -0.5 * LOG2E), jnp.zeros((LANES,), f32)])

    wr = p["w_router"][l].T
    wrh = wr.astype(MXU_DTYPE)
    wrl = (wr - wrh.astype(f32)).astype(MXU_DTYPE)
    bf = lambda a: a.astype(MXU_DTYPE)
    return {
        "g1": p["norm1_g"][l][None, :], "w1": bf(w1), "wqb": bf(wqb), "wkb": bf(wkb), "wvb": bf(wvb),
        "gains": gains, "gbq": p["b_q_a_norm"][l][None, :], "gbkv": p["b_kv_a_norm"][l][None, :],
        "gmem": p["mem_norm_g"][l][None, :], "wmem": bf(p["w_mem_kv"][l]), "gxk": g64(p["x_k_norm"][l])[None, :],
        "sink": p["c_sink"][l],
        "wg": bf(w_in[:, g0:]), "wbr": bf(p["w_branch"][l]), "wo": bf(p["w_out"][l]),
        "g2": p["norm2_g"][l][None, :], "wrh": wrh, "wrl": wrl,
        "wgu": p["w_gate_up"], "wd": p["w_down"], "layer": l,
    }


def _trunk(x, mem, layers):
    batch, seq, _ = x.shape
    n = batch * seq
    tm = ROUTE_CHUNK
    tp = min(512, seq)
    cap = EC_FACTOR * n // N_EXPERTS
    tc = min(512, cap)
    tabs = _rope_tables(seq)
    x2 = x.reshape(n, D_MODEL)
    mem2 = mem.reshape(batch * N_MEM, D_MODEL)
    for lw in layers:
        qa, ka, va, qb, kb, vb, qc, kc, vc, qx = _in_projection(x2, lw, tabs, seq, tp)
        kx, vx = _memory_kv(mem2, lw)
        oa = _flash_attention(qa, ka, va, mode="gqa", batch=batch, seq=seq, kv_len=seq, tq=TQ_A, tk=TK_A)
        ob = _flash_attention(qb, kb, vb, mode="mha", batch=batch, seq=seq, kv_len=seq, tq=TQ_B, tk=TK_B)
        oc = _window_attention(qc, kc, vc, lw["sink"], batch=batch, seq=seq, tq=min(256, seq - 2 * C_WINDOW))
        ox = _flash_attention(qx, kx, vx, mode="cross", batch=batch, seq=seq, kv_len=N_MEM, tq=TQ_X, tk=N_MEM)
        y, h2, aff = _merge(x2, oa, ob, oc, ox, lw, tp)
        sel, ptab = _route(aff, cap)
        xs = _dispatch(ptab, sel, h2, cap, tm)
        ye = _expert_ffn(xs, lw["wgu"], lw["wd"], lw["layer"], cap, tc)
        x2 = _combine(ptab, sel, aff, y, ye, cap, tm)
    return x2.reshape(batch, seq, D_MODEL)


def kernel(x_prompt, x_sample, mem_prompt, mem_sample, norm1_g, w_in, a_q_norm, a_k_norm, b_q_a_norm,
           b_w_q_b, b_kv_a_norm, b_w_kv_b, b_q_norm, b_k_norm, c_q_norm, c_k_norm, c_sink, x_q_norm,
           x_k_norm, mem_norm_g, w_mem_kv, w_branch, w_out, norm2_g, w_router, w_gate_up, w_down):
    p = dict(norm1_g=norm1_g, w_in=w_in, a_q_norm=a_q_norm, a_k_norm=a_k_norm, b_q_a_norm=b_q_a_norm,
             b_w_q_b=b_w_q_b, b_kv_a_norm=b_kv_a_norm, b_w_kv_b=b_w_kv_b, b_q_norm=b_q_norm, b_k_norm=b_k_norm,
             c_q_norm=c_q_norm, c_k_norm=c_k_norm, c_sink=c_sink, x_q_norm=x_q_norm, x_k_norm=x_k_norm,
             mem_norm_g=mem_norm_g, w_mem_kv=w_mem_kv, w_branch=w_branch, w_out=w_out, norm2_g=norm2_g,
             w_router=w_router, w_gate_up=w_gate_up, w_down=w_down)
    layers = [_prep_layer(l, p) for l in range(w_in.shape[0])]
    return (_trunk(x_prompt, mem_prompt, layers), _trunk(x_sample, mem_sample, layers))
```

```python
import functools

import jax
import jax.numpy as jnp
from jax import lax
from jax.experimental import pallas as pl
from jax.experimental.pallas import tpu as pltpu

D_MODEL = 1024
GRID_W = 64
N_MEM = 256
EPS = 1e-6
ROPE_THETA = 10000.0
NEG_INF = -1e30
HEAD = 64
A_KV_HEADS = 2
B_HEADS = 8
B_Q_RANK = 384
B_KV_RANK = 256
B_NOPE = 64
B_ROPE = 32
B_QK = B_NOPE + B_ROPE
C_WINDOW = 128
X_HEADS = 4
N_EXPERTS = 16
EC_FACTOR = 2
D_EXPERT = 1024
LANES = 128

MXU_DTYPE = jnp.bfloat16
VMEM_LIMIT = 56 * 1024 * 1024
LOG2E = 1.4426950408889634
TQ_A, TK_A = 1024, 512
TQ_B, TK_B = 2048, 512
TQ_X = 1024
FLASH_COLS = 256

_C_AQ, _C_AK, _C_AV = 0, 512, 640
_C_BCQ, _C_BCKV = 768, 1152
_C_CQ, _C_CK, _C_CV = 1408, 1920, 2048
_C_XQ, _C_KR, _C_END = 2176, 2432, 2560
V_ROWS = 80


def _dot(a, b):
    return jnp.dot(a, b, preferred_element_type=jnp.float32)


def _dot_nt(a, b):
    return lax.dot_general(a, b, (((1,), (1,)), ((), ())), preferred_element_type=jnp.float32)


def _rms(x, g):
    return x * lax.rsqrt(jnp.mean(x * x, axis=-1, keepdims=True) + EPS) * g


def _block_ones(head_w):
    w = 2 * LANES
    row = lax.broadcasted_iota(jnp.int32, (w, w), 0) // head_w
    col = lax.broadcasted_iota(jnp.int32, (w, w), 1) // head_w
    return jnp.where(row == col, 1.0, 0.0).astype(MXU_DTYPE)


def _head_norm(z, gain, ones, n_valid):
    width = z.shape[1]
    outs = []
    c = 0
    while c < width:
        w = min(2 * LANES, width - c)
        blk = z[:, c:c + w]
        ss = _dot((blk * blk).astype(MXU_DTYPE), ones[:w, :w])
        g = gain if w == LANES else jnp.concatenate([gain, gain], axis=1)
        normed = blk * lax.rsqrt(ss * (1.0 / n_valid) + EPS) * g
        outs += [normed[:, i * LANES:(i + 1) * LANES] for i in range(w // LANES)]
        c += w
    return outs


def _rope(blocks, cos, sin_signed):
    lane = lax.broadcasted_iota(jnp.int32, (1, LANES), 1)
    first = (lane % 32) < 16
    outs = []
    for blk in blocks:
        up = pltpu.roll(blk, LANES - 16, 1)
        dn = pltpu.roll(blk, 16, 1)
        outs.append(blk * cos + jnp.where(first, up, dn) * sin_signed)
    return outs


def _cat(blocks, dtype):
    return jnp.concatenate(blocks, axis=1).astype(dtype)


def _transposed_v(v, dtype):
    vt = v.T
    ones = jnp.ones((V_ROWS - HEAD, v.shape[0]), jnp.float32)
    parts = []
    for h in range(v.shape[1] // HEAD):
        parts += [vt[h * HEAD:(h + 1) * HEAD, :], ones]
    return jnp.concatenate(parts, axis=0).astype(dtype)


def _inproj_kernel(x_ref, g1_ref, w1_ref, wqb_ref, wkb_ref, wvb_ref, gains_ref, gbq_ref, gbkv_ref,
                   cosa_ref, sina_ref, cosb_ref, sinb_ref,
                   qa_ref, ka_ref, va_ref, qb_ref, kb_ref, vb_ref, qc_ref, kc_ref, vc_ref, qx_ref):
    x = x_ref[...]
    hb = _rms(x, g1_ref[...]).astype(MXU_DTYPE)
    z = _dot(hb, w1_ref[...])
    gains = gains_ref[...]
    cosa, sina = cosa_ref[...], sina_ref[...]
    cosb, sinb = cosb_ref[...], sinb_ref[...]
    od = qa_ref.dtype
    ones64, ones128 = _block_ones(HEAD), _block_ones(LANES)

    qa = _rope(_head_norm(z[:, _C_AQ:_C_AK], gains[0:1], ones64, HEAD), cosa, sina)
    qa_ref[...] = _cat(qa, od)
    ka = _rope(_head_norm(z[:, _C_AK:_C_AV], gains[1:2], ones64, HEAD), cosa, sina)
    ka_ref[...] = ka[0].astype(od)
    va_ref[...] = _transposed_v(z[:, _C_AV:_C_BCQ], od)

    cq = _rms(z[:, _C_BCQ:_C_BCKV], gbq_ref[...]).astype(MXU_DTYPE)
    ckv = _rms(z[:, _C_BCKV:_C_CQ], gbkv_ref[...]).astype(MXU_DTYPE)
    q8 = _dot(cq, wqb_ref[...])
    k8 = _dot(ckv, wkb_ref[...])
    kr = z[:, _C_KR:_C_END]
    k8 = k8 + jnp.concatenate([kr] * B_HEADS, axis=1)
    qb_ref[...] = _cat(_rope(_head_norm(q8, gains[2:3], ones128, B_QK), cosb, sinb), od)
    kb_ref[...] = _cat(_rope(_head_norm(k8, gains[3:4], ones128, B_QK), cosb, sinb), od)
    vb_ref[...] = _transposed_v(_dot(ckv, wvb_ref[...]), od)

    qc_ref[...] = _cat(_head_norm(z[:, _C_CQ:_C_CK], gains[4:5], ones64, HEAD), od)
    kc_ref[...] = _head_norm(z[:, _C_CK:_C_CV], gains[5:6], ones64, HEAD)[0].astype(od)
    vc_ref[...] = _transposed_v(z[:, _C_CV:_C_XQ], od)

    qx_ref[...] = _cat(_head_norm(z[:, _C_XQ:_C_KR], gains[6:7], ones64, HEAD), od)


def _in_projection(x2, lw, tabs, seq, tm):
    n = x2.shape[0]
    nt = seq // tm
    row = lambda i: (i, 0)
    full = lambda i: (0, 0)
    tab = lambda i: (i % nt, 0)
    col = lambda i: (0, i)
    outs_def = [(512, False), (LANES, False), (A_KV_HEADS * V_ROWS, True), (1024, False), (1024, False),
                (B_HEADS * V_ROWS, True), (512, False), (LANES, False), (A_KV_HEADS * V_ROWS, True), (256, False)]
    out_specs = [pl.BlockSpec((w, tm), col) if t else pl.BlockSpec((tm, w), row) for w, t in outs_def]
    out_shape = [jax.ShapeDtypeStruct((w, n) if t else (n, w), MXU_DTYPE) for w, t in outs_def]
    outs = pl.pallas_call(
        _inproj_kernel,
        grid=(n // tm,),
        in_specs=[
            pl.BlockSpec((tm, D_MODEL), row),
            pl.BlockSpec((1, D_MODEL), full),
            pl.BlockSpec(lw["w1"].shape, full),
            pl.BlockSpec(lw["wqb"].shape, full),
            pl.BlockSpec(lw["wkb"].shape, full),
            pl.BlockSpec(lw["wvb"].shape, full),
            pl.BlockSpec(lw["gains"].shape, full),
            pl.BlockSpec((1, B_Q_RANK), full),
            pl.BlockSpec((1, B_KV_RANK), full),
            pl.BlockSpec((tm, LANES), tab),
            pl.BlockSpec((tm, LANES), tab),
            pl.BlockSpec((tm, LANES), tab),
            pl.BlockSpec((tm, LANES), tab),
        ],
        out_specs=out_specs,
        out_shape=out_shape,
        compiler_params=pltpu.CompilerParams(dimension_semantics=("parallel",), vmem_limit_bytes=VMEM_LIMIT),
        name="in_projection",
    )(x2, lw["g1"], lw["w1"], lw["wqb"], lw["wkb"], lw["wvb"], lw["gains"], lw["gbq"], lw["gbkv"],
      tabs["cosa"], tabs["sina"], tabs["cosb"], tabs["sinb"])
    return outs


def _memkv_kernel(mem_ref, g_ref, w_ref, gain_ref, k_ref, v_ref):
    m = _rms(mem_ref[...], g_ref[...]).astype(MXU_DTYPE)
    kv = _dot(m, w_ref[...])
    half = kv.shape[1] // 2
    k_ref[...] = _cat(_head_norm(kv[:, :half], gain_ref[...], _block_ones(HEAD), HEAD), k_ref.dtype)
    v_ref[...] = _transposed_v(kv[:, half:], v_ref.dtype)


def _memory_kv(mem2, lw):
    n = mem2.shape[0]
    tm = N_MEM
    xq = X_HEADS * HEAD
    row = lambda i: (i, 0)
    full = lambda i: (0, 0)
    return pl.pallas_call(
        _memkv_kernel,
        grid=(n // tm,),
        in_specs=[pl.BlockSpec((tm, D_MODEL), row), pl.BlockSpec((1, D_MODEL), full),
                  pl.BlockSpec((D_MODEL, 2 * xq), full), pl.BlockSpec((1, LANES), full)],
        out_specs=[pl.BlockSpec((tm, xq), row), pl.BlockSpec((X_HEADS * V_ROWS, tm), lambda i: (0, i))],
        out_shape=[jax.ShapeDtypeStruct((n, xq), MXU_DTYPE),
                   jax.ShapeDtypeStruct((X_HEADS * V_ROWS, n), MXU_DTYPE)],
        compiler_params=pltpu.CompilerParams(dimension_semantics=("parallel",)),
        name="memory_kv",
    )(mem2, lw["gmem"], lw["wmem"], lw["gxk"])


def _flash_kernel(q_ref, k_ref, v_ref, o_ref, qs_ref, s_ref, m_ref, acc_ref, *, mode, nh, tq, tk, n_kv):
    g = pl.program_id(1)
    qt = q_ref[...].astype(jnp.float32).T
    kdim = qs_ref.shape[0]
    for i in range(nh):
        if mode == "gqa":
            blk = qt[i * HEAD:(i + 1) * HEAD, :]
            half = lax.broadcasted_iota(jnp.int32, (kdim, tq), 0) // HEAD
            op = jnp.where(half == g, jnp.concatenate([blk, blk], axis=0), 0.0)
        elif mode == "mha":
            op = qt[i * LANES:(i + 1) * LANES, :]
        else:
            head = lax.broadcasted_iota(jnp.int32, (kdim, tq), 0) // HEAD
            op = jnp.where(head == i, qt, 0.0)
        qs_ref[:, i * tq:(i + 1) * tq] = op.astype(qs_ref.dtype)
    m_ref[...] = jnp.full(m_ref.shape, -jnp.inf, jnp.float32)
    acc_ref[...] = jnp.zeros(acc_ref.shape, jnp.float32)

    ncols = nh * tq
    cw = min(tq if mode == "mha" else ncols, FLASH_COLS)
    chunks = [(c, c + cw) for c in range(0, ncols, cw)]

    def score(j, slot, c0, c1):
        off = pl.multiple_of(j * tk, tk)
        if mode == "mha":
            i = c0 // tq
            s_ref[slot, :, c0:c1] = _dot(k_ref[pl.ds(off, tk), i * LANES:(i + 1) * LANES], qs_ref[:, c0:c1])
        else:
            s_ref[slot, :, c0:c1] = _dot(k_ref[pl.ds(off, tk), :], qs_ref[:, c0:c1])

    def consume(j, slot, c0, c1):
        off = pl.multiple_of(j * tk, tk)
        s = s_ref[slot, :, c0:c1]
        m_prev = m_ref[0:1, c0:c1]
        m_new = jnp.maximum(m_prev, jnp.max(s, axis=0, keepdims=True))
        alpha = jnp.exp2(m_prev - m_new)
        p = jnp.exp2(s - m_new).astype(v_ref.dtype)
        if mode == "gqa":
            pv = _dot(v_ref[:, pl.ds(off, tk)], p)
        else:
            pv = jnp.concatenate(
                [_dot(v_ref[i * V_ROWS:(i + 1) * V_ROWS, pl.ds(off, tk)],
                      p[:, max(i * tq, c0) - c0:min((i + 1) * tq, c1) - c0])
                 for i in range(c0 // tq, (c1 + tq - 1) // tq)], axis=1)
        acc_ref[:, c0:c1] = alpha * acc_ref[:, c0:c1] + pv
        m_ref[0:1, c0:c1] = m_new

    def step(js, slot_s, jc, slot_c):
        for c0, c1 in chunks:
            if js is not None:
                score(js, slot_s, c0, c1)
            if jc is not None:
                consume(jc, slot_c, c0, c1)

    step(0, 0, None, None)
    if n_kv > 1:
        def pair(jj, carry):
            step(2 * jj + 1, 1, 2 * jj, 0)
            step(2 * jj + 2, 0, 2 * jj + 1, 1)
            return carry

        lax.fori_loop(0, n_kv // 2 - 1, pair, 0)
        step(n_kv - 1, 1, n_kv - 2, 0)
        step(None, None, n_kv - 1, 1)
    else:
        step(None, None, 0, 0)
    a = acc_ref[...]
    ot = a[:HEAD, :] / a[HEAD:HEAD + 1, :]
    o_ref[...] = jnp.concatenate([ot[:, i * tq:(i + 1) * tq] for i in range(nh)], axis=0).T.astype(o_ref.dtype)


def _flash_attention(q, k, vt, *, mode, batch, seq, kv_len, tq, tk):
    kw = 256
    groups = q.shape[1] // kw
    nh = {"gqa": 4, "mha": 2, "cross": 4}[mode]
    kcols = {"gqa": LANES, "mha": kw, "cross": kw}[mode]
    kdim = {"gqa": LANES, "mha": LANES, "cross": kw}[mode]
    vrows = V_ROWS if mode == "gqa" else nh * V_ROWS
    k_map = (lambda b, g, i: (b, 0)) if mode == "gqa" else (lambda b, g, i: (b, g))
    tq = min(tq, seq)
    tk = min(tk, kv_len)
    nq = seq // tq
    ow = nh * HEAD
    kern = functools.partial(_flash_kernel, mode=mode, nh=nh, tq=tq, tk=tk, n_kv=kv_len // tk)
    return pl.pallas_call(
        kern,
        grid=(batch, groups, nq),
        in_specs=[pl.BlockSpec((tq, kw), lambda b, g, i: (b * nq + i, g)),
                  pl.BlockSpec((kv_len, kcols), k_map),
                  pl.BlockSpec((vrows, kv_len), lambda b, g, i: (g, b))],
        out_specs=pl.BlockSpec((tq, ow), lambda b, g, i: (b * nq + i, g)),
        out_shape=jax.ShapeDtypeStruct((batch * seq, groups * ow), MXU_DTYPE),
        scratch_shapes=[pltpu.VMEM((kdim, nh * tq), MXU_DTYPE), pltpu.VMEM((2, tk, nh * tq), jnp.float32),
                        pltpu.VMEM((8, nh * tq), jnp.float32), pltpu.VMEM((V_ROWS, nh * tq), jnp.float32)],
        compiler_params=pltpu.CompilerParams(dimension_semantics=("parallel", "parallel", "parallel"),
                                             vmem_limit_bytes=VMEM_LIMIT),
        name=f"flash_attention_{mode}",
    )(q, k, vt)


def _window_kernel(sink_ref, q_ref, k_ref, v_ref, o_ref, *, tq, seq):
    j = pl.program_id(1)
    nh = 8
    win = tq + 2 * C_WINDOW
    start = pl.multiple_of(jnp.clip(j * tq - C_WINDOW, 0, seq - win), C_WINDOW)
    k = k_ref[pl.ds(start, win), :]
    qt = q_ref[...].astype(jnp.float32).T
    half = lax.broadcasted_iota(jnp.int32, (2 * HEAD, tq), 0) // HEAD
    spos = start + lax.broadcasted_iota(jnp.int32, (win, tq), 0)
    tpos = j * tq + lax.broadcasted_iota(jnp.int32, (win, tq), 1)
    rel = jnp.abs(spos - tpos)
    dist = jnp.where(rel <= C_WINDOW, rel.astype(jnp.float32), -NEG_INF)
    outs = []
    for g in range(A_KV_HEADS):
        vt = v_ref[g * V_ROWS:(g + 1) * V_ROWS, pl.ds(start, win)]
        ps, sinks = [], []
        for i in range(nh // A_KV_HEADS):
            h = g * (nh // A_KV_HEADS) + i
            blk = qt[h * HEAD:(h + 1) * HEAD, :]
            op = jnp.where(half == g, jnp.concatenate([blk, blk], axis=0), 0.0).astype(k.dtype)
            si = _dot(k, op) - (2.0 ** -(h + 1) * LOG2E) * dist
            sink = sink_ref[h] * LOG2E
            m = jnp.maximum(jnp.max(si, axis=0, keepdims=True), sink)
            ps.append(jnp.exp2(si - m).astype(vt.dtype))
            sinks.append(jnp.exp2(sink - m))
        acc = _dot(vt, jnp.concatenate(ps, axis=1))
        ot = acc[:HEAD, :] / (acc[HEAD:HEAD + 1, :] + jnp.concatenate(sinks, axis=1))
        outs += [ot[:, i * tq:(i + 1) * tq] for i in range(nh // A_KV_HEADS)]
    o_ref[...] = jnp.concatenate(outs, axis=0).T.astype(o_ref.dtype)


def _window_attention(q, k, v, sink, *, batch, seq, tq):
    width = q.shape[1]
    nq = seq // tq
    kern = functools.partial(_window_kernel, tq=tq, seq=seq)
    return pl.pallas_call(
        kern,
        grid=(batch, nq),
        in_specs=[pl.BlockSpec(memory_space=pltpu.SMEM),
                  pl.BlockSpec((tq, width), lambda b, i: (b * nq + i, 0)),
                  pl.BlockSpec((seq, LANES), lambda b, i: (b, 0)),
                  pl.BlockSpec((A_KV_HEADS * V_ROWS, seq), lambda b, i: (0, b))],
        out_specs=pl.BlockSpec((tq, width), lambda b, i: (b * nq + i, 0)),
        out_shape=jax.ShapeDtypeStruct((batch * seq, width), MXU_DTYPE),
        compiler_params=pltpu.CompilerParams(dimension_semantics=("parallel", "parallel"),
                                             vmem_limit_bytes=VMEM_LIMIT),
        name="window_attention",
    )(sink, q, k, v)


def _merge_kernel(x_ref, oa_ref, ob_ref, oc_ref, ox_ref, g1_ref, wg_ref, wbr_ref, wo_ref, g2_ref,
                  wrh_ref, wrl_ref, y_ref, h2_ref, aff_ref):
    x = x_ref[...]
    hb = _rms(x, g1_ref[...]).astype(MXU_DTYPE)
    merged = jnp.zeros(x.shape, jnp.float32)
    r0 = 0
    for i, o_ref in enumerate((oa_ref, ob_ref, oc_ref, ox_ref)):
        r1 = r0 + o_ref.shape[1]
        gate = jax.nn.sigmoid(_dot(hb, wg_ref[:, i * D_MODEL:(i + 1) * D_MODEL]))
        merged = merged + gate * _dot(o_ref[...], wbr_ref[r0:r1, :])
        r0 = r1
    y = x + _dot(merged.astype(MXU_DTYPE), wo_ref[...])
    y_ref[...] = y
    h2 = _rms(y, g2_ref[...])
    h2_hi = h2.astype(MXU_DTYPE)
    h2_lo = (h2 - h2_hi.astype(jnp.float32)).astype(MXU_DTYPE)
    h2_ref[...] = h2_hi
    wrh, wrl = wrh_ref[...], wrl_ref[...]
    logits = _dot_nt(wrh, h2_hi) + _dot_nt(wrh, h2_lo) + _dot_nt(wrl, h2_hi)
    e = jnp.exp(logits - jnp.max(logits, axis=0, keepdims=True))
    aff_ref[...] = e / jnp.sum(e, axis=0, keepdims=True)


def _merge(x2, oa, ob, oc, ox, lw, tm):
    n = x2.shape[0]
    row = lambda i: (i, 0)
    full = lambda i: (0, 0)
    return pl.pallas_call(
        _merge_kernel,
        grid=(n // tm,),
        in_specs=[pl.BlockSpec((tm, D_MODEL), row),
                  pl.BlockSpec((tm, oa.shape[1]), row), pl.BlockSpec((tm, ob.shape[1]), row),
                  pl.BlockSpec((tm, oc.shape[1]), row), pl.BlockSpec((tm, ox.shape[1]), row),
                  pl.BlockSpec((1, D_MODEL), full),
                  pl.BlockSpec(lw["wg"].shape, full, pipeline_mode=pl.Buffered(1)),
                  pl.BlockSpec(lw["wbr"].shape, full, pipeline_mode=pl.Buffered(1)),
                  pl.BlockSpec(lw["wo"].shape, full, pipeline_mode=pl.Buffered(1)), pl.BlockSpec((1, D_MODEL), full),
                  pl.BlockSpec(lw["wrh"].shape, full), pl.BlockSpec(lw["wrl"].shape, full)],
        out_specs=[pl.BlockSpec((tm, D_MODEL), row), pl.BlockSpec((tm, D_MODEL), row),
                   pl.BlockSpec((N_EXPERTS, tm), lambda i: (0, i))],
        out_shape=[jax.ShapeDtypeStruct((n, D_MODEL), jnp.float32),
                   jax.ShapeDtypeStruct((n, D_MODEL), MXU_DTYPE),
                   jax.ShapeDtypeStruct((N_EXPERTS, n), jnp.float32)],
        compiler_params=pltpu.CompilerParams(dimension_semantics=("parallel",), vmem_limit_bytes=VMEM_LIMIT),
        name="merge_router",
    )(x2, oa, ob, oc, ox, lw["g1"], lw["wg"], lw["wbr"], lw["wo"], lw["g2"], lw["wrh"], lw["wrl"])


ROW_TILE = (8, LANES)


def _expert_kernel(xs_ref, wgu_ref, wd_ref, y_ref, wgu_bf, wd_bf):
    tc = xs_ref.shape[1]

    @pl.when(pl.program_id(1) == 0)
    def _():
        wgu_bf[...] = wgu_ref[0, 0].astype(wgu_bf.dtype)
        wd_bf[...] = wd_ref[0, 0].astype(wd_bf.dtype)

    gu = _dot(xs_ref[0].reshape(tc, D_MODEL), wgu_bf[...])
    g, u = gu[:, :D_EXPERT], gu[:, D_EXPERT:]
    act = (g * jax.nn.sigmoid(g) * u).astype(MXU_DTYPE)
    y_ref[0] = _dot(act, wd_bf[...]).astype(y_ref.dtype).reshape((tc,) + ROW_TILE)


def _expert_ffn(xs, wgu, wd, layer, cap, tc):
    nt = cap // tc
    return pl.pallas_call(
        _expert_kernel,
        grid=(N_EXPERTS, nt),
        in_specs=[pl.BlockSpec((1, tc) + ROW_TILE, lambda e, c: (e, c, 0, 0)),
                  pl.BlockSpec((1, 1, D_MODEL, 2 * D_EXPERT), lambda e, c: (layer, e, 0, 0)),
                  pl.BlockSpec((1, 1, D_EXPERT, D_MODEL), lambda e, c: (layer, e, 0, 0))],
        out_specs=pl.BlockSpec((1, tc) + ROW_TILE, lambda e, c: (e, c, 0, 0)),
        out_shape=jax.ShapeDtypeStruct((N_EXPERTS, cap) + ROW_TILE, MXU_DTYPE),
        scratch_shapes=[pltpu.VMEM((D_MODEL, 2 * D_EXPERT), MXU_DTYPE), pltpu.VMEM((D_EXPERT, D_MODEL), MXU_DTYPE)],
        compiler_params=pltpu.CompilerParams(dimension_semantics=("arbitrary", "arbitrary"),
                                             vmem_limit_bytes=VMEM_LIMIT),
        name="expert_ffn",
    )(xs, wgu, wd)


CHUNK = 32
GROUP = 512
MAX_BLOCKS = 128
ROUTE_CHUNK = 256


def _route_kernel(aff_ref, sel_ref, ptab_ref, *, cap, n):
    bits = lax.bitcast_convert_type(aff_ref[...], jnp.int32)

    def search(b, thr):
        cand = thr | jnp.left_shift(jnp.int32(1), 30 - b)
        cnt = jnp.sum((bits >= cand).astype(jnp.int32), axis=1, keepdims=True)
        return jnp.where(cnt >= cap, cand, thr)

    thr = lax.fori_loop(0, 31, search, jnp.zeros((N_EXPERTS, 1), jnp.int32))
    n_gt = jnp.sum((bits > thr).astype(jnp.int32), axis=1, keepdims=True)
    need = (cap - n_gt).astype(jnp.float32)
    k = lax.broadcasted_iota(jnp.int32, (ROUTE_CHUNK, ROUTE_CHUNK), 0)
    j = lax.broadcasted_iota(jnp.int32, (ROUTE_CHUNK, ROUTE_CHUNK), 1)
    tri = jnp.where(k <= j, 1.0, 0.0).astype(MXU_DTYPE)
    tile = lax.broadcasted_iota(jnp.int32, ptab_ref.shape, 1)

    def chunk(c, carry):
        cg, ce, tab = carry
        off = pl.multiple_of(c * ROUTE_CHUNK, ROUTE_CHUNK)
        tab = jnp.where(tile == c, cg + jnp.minimum(ce, need), tab)
        b = lax.bitcast_convert_type(aff_ref[:, pl.ds(off, ROUTE_CHUNK)], jnp.int32)
        gt = jnp.where(b > thr, 1.0, 0.0)
        eq = jnp.where(b == thr, 1.0, 0.0)
        pre = _dot(jnp.concatenate([gt, eq], axis=0).astype(MXU_DTYPE), tri)
        pg = pre[:N_EXPERTS] + cg
        pe = pre[N_EXPERTS:] + ce
        before = (pg - gt) + jnp.minimum(pe - eq, need)
        chosen = (gt > 0.0) | ((eq > 0.0) & ((pe - eq) < need))
        sel_ref[:, pl.ds(off, ROUTE_CHUNK)] = jnp.where(chosen, before.astype(jnp.int32), -1)
        return pg[:, ROUTE_CHUNK - 1:ROUTE_CHUNK], pe[:, ROUTE_CHUNK - 1:ROUTE_CHUNK], tab

    zero = jnp.zeros((N_EXPERTS, 1), jnp.float32)
    n_tiles = n // ROUTE_CHUNK
    _, _, tab = lax.fori_loop(0, n_tiles, chunk, (zero, zero, jnp.zeros(ptab_ref.shape, jnp.float32)))
    ptab_ref[...] = jnp.where(tile == n_tiles, cap, tab.astype(jnp.int32))


def _route(aff, cap):
    n = aff.shape[1]
    ntp = -(-(n // ROUTE_CHUNK + 1) // LANES) * LANES
    full = lambda i: (0, 0)
    return pl.pallas_call(
        functools.partial(_route_kernel, cap=cap, n=n),
        grid=(1,),
        in_specs=[pl.BlockSpec((N_EXPERTS, n), full)],
        out_specs=[pl.BlockSpec((N_EXPERTS, n), full), pl.BlockSpec((N_EXPERTS, ntp), full)],
        out_shape=[jax.ShapeDtypeStruct((N_EXPERTS, n), jnp.int32),
                   jax.ShapeDtypeStruct((N_EXPERTS, ntp), jnp.int32)],
        compiler_params=pltpu.CompilerParams(dimension_semantics=("arbitrary",), vmem_limit_bytes=VMEM_LIMIT),
        name="route_select",
    )(aff)


def _block_table(ptab_ref, i, tab_e, tab_base, tab_lo, slot, limit):
    def per_expert(e, nb):
        p0 = ptab_ref[e, i]
        nblk = jnp.right_shift(ptab_ref[e, i + 1] - p0 + (CHUNK - 1), 5)

        def per_block(kk, b):
            tab_e[slot, b] = e
            tab_lo[slot, b] = p0 + kk * CHUNK
            tab_base[slot, b] = jnp.minimum(p0 + kk * CHUNK, limit)
            return b + 1

        return lax.fori_loop(0, nblk, per_block, nb)

    nb = jnp.int32(0)
    for e in range(N_EXPERTS):
        nb = per_expert(e, nb)
    return nb


def _dispatch_kernel(ptab_ref, sel_ref, h_ref, xs_ref, oh_ref, res_ref, tab_e, tab_base, tab_lo, cnt_ref, sem, *,
                     cap, tm):
    i = pl.program_id(0)
    last = pl.num_programs(0) - 1

    def chunk_copy(b):
        e = tab_e[0, b]
        base = tab_base[0, b]
        per = GROUP // CHUNK
        src = res_ref.at[b // per, pl.ds(pl.multiple_of(lax.rem(b, per) * CHUNK, CHUNK), CHUNK)]
        return pltpu.make_async_copy(src, xs_ref.at[e, pl.ds(base, CHUNK)], sem.at[0])

    def wait_all(nb):
        def w(b, c):
            chunk_copy(b).wait()
            return c
        lax.fori_loop(0, nb, w, 0)

    @pl.when(i == 0)
    def _():
        oh_ref[...] = jnp.zeros(oh_ref.shape, oh_ref.dtype)
        cnt_ref[0] = 0
        res_ref[0] = jnp.zeros(res_ref.shape[1:], res_ref.dtype)
        pads = [pltpu.make_async_copy(res_ref.at[0, pl.ds(0, CHUNK)], xs_ref.at[e, pl.ds(cap, CHUNK)], sem.at[0])
                for e in range(N_EXPERTS)]
        for c in pads:
            c.start()
        for c in pads:
            c.wait()

    wait_all(cnt_ref[0])
    nb = _block_table(ptab_ref, i, tab_e, tab_base, tab_lo, 0, cap)

    def build(b, c):
        e = tab_e[0, b]
        slots = tab_base[0, b] + lax.broadcasted_iota(jnp.int32, (CHUNK, tm), 0)
        hit = sel_ref[pl.ds(e, 1), :] == slots
        oh_ref[pl.ds(pl.multiple_of(b * CHUNK, CHUNK), CHUNK), :] = jnp.where(hit, 1.0, 0.0).astype(oh_ref.dtype)
        return c

    lax.fori_loop(0, nb, build, 0)

    def group(gi, c):
        rows = pl.ds(pl.multiple_of(gi * GROUP, GROUP), GROUP)
        res_ref[gi] = _dot(oh_ref[rows, :], h_ref[...]).astype(res_ref.dtype).reshape((GROUP,) + ROW_TILE)
        return c

    lax.fori_loop(0, jnp.right_shift(nb * CHUNK + GROUP - 1, GROUP.bit_length() - 1), group, 0)

    def issue(b, c):
        chunk_copy(b).start()
        return c

    lax.fori_loop(0, nb, issue, 0)
    cnt_ref[0] = nb

    @pl.when(i == last)
    def _():
        wait_all(nb)
        cnt_ref[0] = 0


def _dispatch(ptab, sel, h2, cap, tm):
    n = h2.shape[0]
    grid_spec = pltpu.PrefetchScalarGridSpec(
        num_scalar_prefetch=1,
        grid=(n // tm,),
        in_specs=[pl.BlockSpec((N_EXPERTS, tm), lambda i, p: (0, i)),
                  pl.BlockSpec((tm, D_MODEL), lambda i, p: (i, 0))],
        out_specs=pl.BlockSpec(memory_space=pl.ANY),
        scratch_shapes=[pltpu.VMEM((MAX_BLOCKS * CHUNK, tm), MXU_DTYPE),
                        pltpu.VMEM((MAX_BLOCKS * CHUNK // GROUP, GROUP) + ROW_TILE, MXU_DTYPE),
                        pltpu.SMEM((1, MAX_BLOCKS), jnp.int32), pltpu.SMEM((1, MAX_BLOCKS), jnp.int32),
                        pltpu.SMEM((1, MAX_BLOCKS), jnp.int32),
                        pltpu.SMEM((1,), jnp.int32), pltpu.SemaphoreType.DMA((1,))],
    )
    return pl.pallas_call(
        functools.partial(_dispatch_kernel, cap=cap, tm=tm),
        grid_spec=grid_spec,
        out_shape=jax.ShapeDtypeStruct((N_EXPERTS, cap + CHUNK) + ROW_TILE, MXU_DTYPE),
        compiler_params=pltpu.CompilerParams(dimension_semantics=("arbitrary",), vmem_limit_bytes=VMEM_LIMIT),
        name="moe_dispatch",
    )(ptab, sel, h2)


def _combine_kernel(ptab_ref, sel_ref, aff_ref, y_ref, ye_ref, o_ref, stage_ref, wt_ref, tab_e, tab_base, tab_lo,
                    cnt_ref, sem, *, cap, tm):
    i = pl.program_id(0)
    nt = pl.num_programs(0)
    slot = lax.rem(i, 2)

    def chunk_copy(s, b):
        e = tab_e[s, b]
        base = tab_base[s, b]
        dst = stage_ref.at[s, pl.ds(pl.multiple_of(b * CHUNK, CHUNK), CHUNK)]
        return pltpu.make_async_copy(ye_ref.at[e, pl.ds(base, CHUNK)], dst, sem.at[s])

    def fetch(tile, s):
        nb = _block_table(ptab_ref, tile, tab_e, tab_base, tab_lo, s, cap - CHUNK)
        cnt_ref[s] = nb

        def issue(b, c):
            chunk_copy(s, b).start()
            return c

        lax.fori_loop(0, nb, issue, 0)

    @pl.when(i == 0)
    def _():
        stage_ref[...] = jnp.zeros(stage_ref.shape, stage_ref.dtype)
        fetch(i, slot)

    @pl.when(i + 1 < nt)
    def _():
        fetch(i + 1, 1 - slot)

    nb = cnt_ref[slot]

    def wait(b, c):
        chunk_copy(slot, b).wait()
        return c

    lax.fori_loop(0, nb, wait, 0)
    ngroups = jnp.right_shift(nb * CHUNK + GROUP - 1, GROUP.bit_length() - 1)

    def build(b, c):
        e = tab_e[slot, b]
        slots = tab_base[slot, b] + lax.broadcasted_iota(jnp.int32, (CHUNK, tm), 0)
        hit = (sel_ref[pl.ds(e, 1), :] == slots) & (slots >= tab_lo[slot, b])
        w = jnp.where(hit, aff_ref[pl.ds(e, 1), :], 0.0)
        wt_ref[pl.ds(pl.multiple_of(b * CHUNK, CHUNK), CHUNK), :] = w.astype(wt_ref.dtype)
        return c

    def clear(b, c):
        wt_ref[pl.ds(pl.multiple_of(b * CHUNK, CHUNK), CHUNK), :] = jnp.zeros((CHUNK, tm), wt_ref.dtype)
        return c

    lax.fori_loop(0, nb, build, 0)
    lax.fori_loop(nb, ngroups * (GROUP // CHUNK), clear, 0)
    o_ref[...] = y_ref[...]

    def group(gi, c):
        rows = pl.ds(pl.multiple_of(gi * GROUP, GROUP), GROUP)
        o_ref[...] += lax.dot_general(wt_ref[rows, :], stage_ref[slot, rows].reshape(GROUP, D_MODEL),
                                      (((0,), (0,)), ((), ())), preferred_element_type=jnp.float32)
        return c

    lax.fori_loop(0, ngroups, group, 0)


def _combine(ptab, sel, aff, y, ye, cap, tm):
    n = y.shape[0]
    grid_spec = pltpu.PrefetchScalarGridSpec(
        num_scalar_prefetch=1,
        grid=(n // tm,),
        in_specs=[pl.BlockSpec((N_EXPERTS, tm), lambda i, p: (0, i)),
                  pl.BlockSpec((N_EXPERTS, tm), lambda i, p: (0, i)),
                  pl.BlockSpec((tm, D_MODEL), lambda i, p: (i, 0)),
                  pl.BlockSpec(memory_space=pl.ANY)],
        out_specs=pl.BlockSpec((tm, D_MODEL), lambda i, p: (i, 0)),
        scratch_shapes=[pltpu.VMEM((2, MAX_BLOCKS * CHUNK) + ROW_TILE, MXU_DTYPE),
                        pltpu.VMEM((MAX_BLOCKS * CHUNK, tm), MXU_DTYPE),
                        pltpu.SMEM((2, MAX_BLOCKS), jnp.int32), pltpu.SMEM((2, MAX_BLOCKS), jnp.int32),
                        pltpu.SMEM((2, MAX_BLOCKS), jnp.int32),
                        pltpu.SMEM((2,), jnp.int32), pltpu.SemaphoreType.DMA((2,))],
    )
    return pl.pallas_call(
        functools.partial(_combine_kernel, cap=cap, tm=tm),
        grid_spec=grid_spec,
        out_shape=jax.ShapeDtypeStruct((n, D_MODEL), jnp.float32),
        compiler_params=pltpu.CompilerParams(dimension_semantics=("arbitrary",), vmem_limit_bytes=VMEM_LIMIT),
        name="moe_combine",
    )(ptab, sel, aff, y, ye)


def _rope_tables(seq):
    pos = jnp.arange(seq, dtype=jnp.int32)
    inv = jnp.power(ROPE_THETA, -jnp.arange(0, 32, 2, dtype=jnp.float32) / 32.0)
    ang_row = (pos // GRID_W).astype(jnp.float32)[:, None] * inv[None, :]
    ang_col = (pos % GRID_W).astype(jnp.float32)[:, None] * inv[None, :]
    ang_seq = pos.astype(jnp.float32)[:, None] * inv[None, :]
    sign = jnp.concatenate([-jnp.ones((16,), jnp.float32), jnp.ones((16,), jnp.float32)])

    def pair(ang):
        return jnp.tile(jnp.cos(ang), (1, 2)), jnp.tile(jnp.sin(ang), (1, 2)) * sign[None, :]

    cr, sr = pair(ang_row)
    cc, sc = pair(ang_col)
    cs, ss = pair(ang_seq)
    one = jnp.ones((seq, 32), jnp.float32)
    zero = jnp.zeros((seq, 32), jnp.float32)
    return {
        "cosa": jnp.concatenate([cr, cc, cr, cc], axis=1),
        "sina": jnp.concatenate([sr, sc, sr, sc], axis=1),
        "cosb": jnp.concatenate([one, one, cs, one], axis=1),
        "sinb": jnp.concatenate([zero, zero, ss, zero], axis=1),
    }


def _prep_layer(l, p):
    f32 = jnp.float32
    w_in = p["w_in"][l]
    a0, b0, c0, x0, g0 = 0, 768, 1440, 2208, 2464

    kr = jnp.zeros((D_MODEL, LANES), f32).at[:, 64:96].set(w_in[:, b0 + 640:b0 + 672])
    w1 = jnp.concatenate([
        w_in[:, a0:a0 + 768],
        w_in[:, b0:b0 + 384], w_in[:, b0 + 384:b0 + 640],
        w_in[:, c0:c0 + 768],
        w_in[:, x0:x0 + 256], kr], axis=1)
    assert w1.shape[1] == _C_END

    wqb = jnp.zeros((B_Q_RANK, B_HEADS, LANES), f32).at[:, :, :B_QK].set(
        p["b_w_q_b"][l].reshape(B_Q_RANK, B_HEADS, B_QK)).reshape(B_Q_RANK, B_HEADS * LANES)
    wkv = p["b_w_kv_b"][l].reshape(B_KV_RANK, B_HEADS, 2 * B_NOPE)
    wkb = jnp.zeros((B_KV_RANK, B_HEADS, LANES), f32).at[:, :, :B_NOPE].set(
        wkv[:, :, :B_NOPE]).reshape(B_KV_RANK, B_HEADS * LANES)
    wvb = wkv[:, :, B_NOPE:].reshape(B_KV_RANK, B_HEADS * B_NOPE)

    def g64(g, scale=1.0):
        return jnp.tile(g * scale, 2)

    def g96(g, scale=1.0):
        return jnp.concatenate([g * scale, jnp.zeros((LANES - B_QK,), f32)])

    gains = jnp.stack([
        g64(p["a_q_norm"][l], HEAD ** -0.5 * LOG2E), g64(p["a_k_norm"][l]),
        g96(p["b_q_norm"][l], B_QK ** -0.5 * LOG2E), g96(p["b_k_norm"][l]),
        g64(p["c_q_norm"][l], HEAD ** -0.5 * LOG2E), g64(p["c_k_norm"][l]),
        g64(p["x_q_norm"][l], HEAD ** -0.5 * LOG2E), jnp.zeros((LANES,), f32)])

    wr = p["w_router"][l].T
    wrh = wr.astype(MXU_DTYPE)
    wrl = (wr - wrh.astype(f32)).astype(MXU_DTYPE)
    bf = lambda a: a.astype(MXU_DTYPE)
    return {
        "g1": p["norm1_g"][l][None, :], "w1": bf(w1), "wqb": bf(wqb), "wkb": bf(wkb), "wvb": bf(wvb),
        "gains": gains, "gbq": p["b_q_a_norm"][l][None, :], "gbkv": p["b_kv_a_norm"][l][None, :],
        "gmem": p["mem_norm_g"][l][None, :], "wmem": bf(p["w_mem_kv"][l]), "gxk": g64(p["x_k_norm"][l])[None, :],
        "sink": p["c_sink"][l],
        "wg": bf(w_in[:, g0:]), "wbr": bf(p["w_branch"][l]), "wo": bf(p["w_out"][l]),
        "g2": p["norm2_g"][l][None, :], "wrh": wrh, "wrl": wrl,
        "wgu": p["w_gate_up"], "wd": p["w_down"], "layer": l,
    }


def _trunk(x, mem, layers):
    batch, seq, _ = x.shape
    n = batch * seq
    tm = ROUTE_CHUNK
    tp = min(512, seq)
    cap = EC_FACTOR * n // N_EXPERTS
    tc = min(512, cap)
    tabs = _rope_tables(seq)
    x2 = x.reshape(n, D_MODEL)
    mem2 = mem.reshape(batch * N_MEM, D_MODEL)
    for lw in layers:
        qa, ka, va, qb, kb, vb, qc, kc, vc, qx = _in_projection(x2, lw, tabs, seq, tp)
        kx, vx = _memory_kv(mem2, lw)
        oa = _flash_attention(qa, ka, va, mode="gqa", batch=batch, seq=seq, kv_len=seq, tq=TQ_A, tk=TK_A)
        ob = _flash_attention(qb, kb, vb, mode="mha", batch=batch, seq=seq, kv_len=seq, tq=TQ_B, tk=TK_B)
        oc = _window_attention(qc, kc, vc, lw["sink"], batch=batch, seq=seq, tq=min(256, seq - 2 * C_WINDOW))
        ox = _flash_attention(qx, kx, vx, mode="cross", batch=batch, seq=seq, kv_len=N_MEM, tq=TQ_X, tk=N_MEM)
        y, h2, aff = _merge(x2, oa, ob, oc, ox, lw, tp)
        sel, ptab = _route(aff, cap)
        xs = _dispatch(ptab, sel, h2, cap, tm)
        ye = _expert_ffn(xs, lw["wgu"], lw["wd"], lw["layer"], cap, tc)
        x2 = _combine(ptab, sel, aff, y, ye, cap, tm)
    return x2.reshape(batch, seq, D_MODEL)


def kernel(x_prompt, x_sample, mem_prompt, mem_sample, norm1_g, w_in, a_q_norm, a_k_norm, b_q_a_norm,
           b_w_q_b, b_kv_a_norm, b_w_kv_b, b_q_norm, b_k_norm, c_q_norm, c_k_norm, c_sink, x_q_norm,
           x_k_norm, mem_norm_g, w_mem_kv, w_branch, w_out, norm2_g, w_router, w_gate_up, w_down):
    p = dict(norm1_g=norm1_g, w_in=w_in, a_q_norm=a_q_norm, a_k_norm=a_k_norm, b_q_a_norm=b_q_a_norm,
             b_w_q_b=b_w_q_b, b_kv_a_norm=b_kv_a_norm, b_w_kv_b=b_w_kv_b, b_q_norm=b_q_norm, b_k_norm=b_k_norm,
             c_q_norm=c_q_norm, c_k_norm=c_k_norm, c_sink=c_sink, x_q_norm=x_q_norm, x_k_norm=x_k_norm,
             mem_norm_g=mem_norm_g, w_mem_kv=w_mem_kv, w_branch=w_branch, w_out=w_out, norm2_g=norm2_g,
             w_router=w_router, w_gate_up=w_gate_up, w_down=w_down)
    layers = [_prep_layer(l, p) for l in range(w_in.shape[0])]
    return (_trunk(x_prompt, mem_prompt, layers), _trunk(x_sample, mem_sample, layers))
```
